```python
import jax, jax.numpy as jnp
from jax import lax
import numpy as np

D_MODEL = 2048
BATCH = 4
SEQ = 8192
DEPTH = 1

MEM_LEN = 256
HEAD_DIM = 128
H_HGRN = 6
H_FOX = 6
H_MEM = 4
W_HGRN = H_HGRN * HEAD_DIM
W_FOX = H_FOX * HEAD_DIM
W_MEM = H_MEM * HEAD_DIM
D_FF = 128 * ((8 * D_MODEL // 3 + 127) // 128)
CHUNK = 64
Q_BLOCK = 128
N_BRANCH = 3
EPS = 1e-6
IN_SIZES = (W_HGRN, W_HGRN, W_HGRN, W_HGRN,
            W_FOX, W_FOX, W_FOX, H_FOX,
            W_MEM)
IN_COLS = sum(IN_SIZES)
IN_SPLITS = tuple(int(s) for s in np.cumsum(IN_SIZES)[:-1])

kernel_name = "hybrid_hgrn2_fox_mem_macaron"


def rms_norm(x, g):
    xf = x.astype(jnp.float32)
    y = xf * lax.rsqrt(jnp.mean(xf * xf, axis=-1, keepdims=True) + EPS)
    return (y * g.astype(jnp.float32)).astype(x.dtype)


def swiglu(x, w_gate, w_up, w_down):
    return (jax.nn.silu(x @ w_gate) * (x @ w_up)) @ w_down


def split_heads(a, n):
    b, t, _ = a.shape
    return a.reshape(b, t, n, -1).transpose(0, 2, 1, 3)


def merge_heads(a):
    b, h, t, d = a.shape
    return a.transpose(0, 2, 1, 3).reshape(b, t, h * d)


def hgrn2_chunk_scan(q, k, v, log_f):
    bsz, h, t, dk = q.shape
    dv = v.shape[-1]
    n = t // CHUNK

    def to_chunks(a):
        return jnp.moveaxis(a.reshape(bsz, h, n, CHUNK, a.shape[-1]), 2, 0)

    qc, kc, vc, gc = to_chunks(q), to_chunks(k), to_chunks(v), to_chunks(log_f)
    causal = jnp.tril(jnp.ones((CHUNK, CHUNK), dtype=bool))[:, :, None]

    def step(state, inp):
        qb, kb, vb, gb = inp
        bcum = jnp.cumsum(gb, axis=2)
        o_inter = jnp.einsum('bhtk,bhkv->bhtv', qb * jnp.exp(bcum), state)
        diff = bcum[:, :, :, None, :] - bcum[:, :, None, :, :]
        decay = jnp.exp(jnp.where(causal, diff, -jnp.inf))
        scores = jnp.einsum('bhtk,bhtsk->bhts', qb, decay * kb[:, :, None, :, :])
        o_intra = jnp.einsum('bhts,bhsv->bhtv', scores, vb)
        last = bcum[:, :, -1:, :]
        k_to_end = kb * jnp.exp(last - bcum)
        new_state = jnp.exp(last[:, :, 0, :])[..., None] * state + \
            jnp.einsum('bhsk,bhsv->bhkv', k_to_end, vb)
        return new_state, o_inter + o_intra

    s0 = jnp.zeros((bsz, h, dk, dv), jnp.float32)
    _, o = lax.scan(step, s0, (qc, kc, vc, gc))
    return jnp.moveaxis(o, 0, 2).reshape(bsz, h, t, dv)


def forgetting_attention(q, k, v, log_f):
    bsz, h, t, d = q.shape
    c = jnp.cumsum(log_f, axis=-1)
    nb = t // Q_BLOCK
    q_blocks = jnp.moveaxis(q.reshape(bsz, h, nb, Q_BLOCK, d), 2, 0)
    c_blocks = jnp.moveaxis(c.reshape(bsz, h, nb, Q_BLOCK), 2, 0)
    starts = jnp.arange(nb, dtype=jnp.int32) * Q_BLOCK
    key_pos = jnp.arange(t, dtype=jnp.int32)
    scale = d ** -0.5

    def block(args):
        q_blk, c_blk, start = args
        s = jnp.einsum('bhqd,bhkd->bhqk', q_blk, k).astype(jnp.float32) * scale
        s = s + c_blk[..., None] - c[:, :, None, :]
        q_pos = start + jnp.arange(Q_BLOCK, dtype=jnp.int32)
        s = jnp.where(key_pos[None, :] <= q_pos[:, None], s, -jnp.inf)
        p = jax.nn.softmax(s, axis=-1).astype(v.dtype)
        return jnp.einsum('bhqk,bhkd->bhqd', p, v)

    o = lax.map(block, (q_blocks, c_blocks, starts))
    return jnp.moveaxis(o, 0, 2).reshape(bsz, h, t, d)


def memory_attention(q, mem_k, mem_v):
    s = jnp.einsum('bhtd,bhmd->bhtm', q, mem_k).astype(jnp.float32) * (HEAD_DIM ** -0.5)
    p = jax.nn.softmax(s, axis=-1).astype(mem_v.dtype)
    return jnp.einsum('bhtm,bhmd->bhtd', p, mem_v)


def hybrid_mixer(u, mem, mem_g, w_in, lb, hgrn_g, fox_b, w_mem_kv,
                 w_hgrn_out, w_fox_out, w_mem_out, w_gate, w_o):
    bsz, t, _ = u.shape
    proj = u @ w_in
    (hq, hf, hi, hog, fq, fk, fv, ff, mq) = jnp.split(proj, IN_SPLITS, axis=-1)

    lbf = lb.astype(jnp.float32)
    g = lbf + (1.0 - lbf) * jax.nn.sigmoid(hf.astype(jnp.float32))
    log_g = jnp.log(g)
    o_h = hgrn2_chunk_scan(split_heads(jax.nn.silu(hq).astype(jnp.float32), H_HGRN),
                           split_heads(1.0 - g, H_HGRN),
                           split_heads(hi.astype(jnp.float32), H_HGRN),
                           split_heads(log_g, H_HGRN))
    o_h = o_h * lax.rsqrt(jnp.mean(o_h * o_h, axis=-1, keepdims=True) + EPS)
    o_h = o_h * hgrn_g.astype(jnp.float32).reshape(H_HGRN, HEAD_DIM)[None, :, None, :]
    o_h = merge_heads(o_h).astype(u.dtype) * jax.nn.silu(hog)

    fox_logf = jax.nn.log_sigmoid(ff.astype(jnp.float32) + fox_b.astype(jnp.float32))
    o_f = forgetting_attention(split_heads(fq, H_FOX), split_heads(fk, H_FOX),
                               split_heads(fv, H_FOX), fox_logf.transpose(0, 2, 1))
    o_f = merge_heads(o_f)

    mem_kv = rms_norm(mem, mem_g) @ w_mem_kv
    mk, mv = jnp.split(mem_kv, 2, axis=-1)
    o_m = merge_heads(memory_attention(split_heads(mq, H_MEM),
                                       split_heads(mk, H_MEM), split_heads(mv, H_MEM)))

    gates = jax.nn.sigmoid(u @ w_gate).reshape(bsz, t, N_BRANCH, D_MODEL)
    merged = gates[:, :, 0] * (o_h @ w_hgrn_out) + \
        gates[:, :, 1] * (o_f @ w_fox_out) + \
        gates[:, :, 2] * (o_m @ w_mem_out)
    return merged @ w_o


def setup_inputs(seed: int = 0) -> dict:
    key = jax.random.key(seed)
    ks = jax.random.split(key, 32)
    L = DEPTH

    def dense(k, fan_in, fan_out):
        return jax.random.normal(k, (L, fan_in, fan_out), jnp.float32) * fan_in ** -0.5

    def gain(k, n):
        return 1.0 + 0.05 * jax.random.normal(k, (L, n), jnp.float32)

    return {
        "x": jax.random.normal(ks[0], (BATCH, SEQ, D_MODEL), jnp.float32),
        "mem": jax.random.normal(ks[1], (BATCH, MEM_LEN, D_MODEL), jnp.float32),
        "ffn1_pre": gain(ks[2], D_MODEL),
        "ffn1_post": gain(ks[3], D_MODEL),
        "ffn1_wg": dense(ks[4], D_MODEL, D_FF),
        "ffn1_wu": dense(ks[5], D_MODEL, D_FF),
        "ffn1_wd": dense(ks[6], D_FF, D_MODEL),
        "mix_pre": gain(ks[7], D_MODEL),
        "mix_post": gain(ks[8], D_MODEL),
        "mem_norm": gain(ks[9], D_MODEL),
        "w_in": dense(ks[10], D_MODEL, IN_COLS),
        "hgrn_lb": 0.1 * jax.random.normal(ks[11], (DEPTH + 1, W_HGRN), jnp.float32),
        "hgrn_gnorm": gain(ks[12], W_HGRN),
        "fox_fb": 1.0 + 0.1 * jax.random.normal(ks[13], (L, H_FOX), jnp.float32),
        "w_mem_kv": dense(ks[14], D_MODEL, 2 * W_MEM),
        "w_hgrn_out": dense(ks[15], W_HGRN, D_MODEL),
        "w_fox_out": dense(ks[16], W_FOX, D_MODEL),
        "w_mem_out": dense(ks[17], W_MEM, D_MODEL),
        "w_gate": dense(ks[18], D_MODEL, N_BRANCH * D_MODEL),
        "w_o": dense(ks[19], D_MODEL, D_MODEL),
        "ffn2_pre": gain(ks[20], D_MODEL),
        "ffn2_post": gain(ks[21], D_MODEL),
        "ffn2_wg": dense(ks[22], D_MODEL, D_FF),
        "ffn2_wu": dense(ks[23], D_MODEL, D_FF),
        "ffn2_wd": dense(ks[24], D_FF, D_MODEL),
    }


def reference(x, mem, ffn1_pre, ffn1_post, ffn1_wg, ffn1_wu, ffn1_wd,
              mix_pre, mix_post, mem_norm, w_in, hgrn_lb, hgrn_gnorm, fox_fb,
              w_mem_kv, w_hgrn_out, w_fox_out, w_mem_out, w_gate, w_o,
              ffn2_pre, ffn2_post, ffn2_wg, ffn2_wu, ffn2_wd):
    lb_all = jnp.cumsum(jax.nn.softmax(hgrn_lb.astype(jnp.float32), axis=0), axis=0)
    for l in range(DEPTH):
        h = swiglu(rms_norm(x, ffn1_pre[l]), ffn1_wg[l], ffn1_wu[l], ffn1_wd[l])
        x = x + 0.5 * rms_norm(h, ffn1_post[l])
        m = hybrid_mixer(rms_norm(x, mix_pre[l]), mem, mem_norm[l], w_in[l], lb_all[l],
                         hgrn_gnorm[l], fox_fb[l], w_mem_kv[l], w_hgrn_out[l],
                         w_fox_out[l], w_mem_out[l], w_gate[l], w_o[l])
        x = x + rms_norm(m, mix_post[l])
        h = swiglu(rms_norm(x, ffn2_pre[l]), ffn2_wg[l], ffn2_wu[l], ffn2_wd[l])
        x = x + 0.5 * rms_norm(h, ffn2_post[l])
    return x
```

```python
import functools
import math

import numpy as np
import jax
import jax.numpy as jnp
from jax import lax
from jax.experimental import pallas as pl
from jax.experimental.pallas import tpu as pltpu

F32 = jnp.float32
BF16 = jnp.bfloat16
EPS = 1e-6
HEAD_DIM = 128
LANES = 128
LOG2E = math.log2(math.e)
VMEM_LIMIT_BYTES = 56 * 1024 * 1024
NEG_BIG = -1e30


def _tiles(n_tokens, seq, proj_linear_cols, proj_gate_cols, ff_col):
    proj_tn = next(w for w in (1024, 512, 256, 128)
                   if proj_linear_cols % w == 0 and proj_gate_cols % w == 0
                   and ff_col % w + LANES <= w)
    return dict(
        ffn_tm=min(512, n_tokens), ffn_tf=512,
        proj_tm=min(1024, n_tokens), proj_tn=proj_tn,
        hgrn_chunk=min(128, seq), hgrn_tt=min(512, seq),
        prep_tt=min(256, seq),
        fox_t=min(512, seq),
        mem_tq=min(512, seq),
        merge_tm=min(256, n_tokens),
    )


def _params(sem):
    return pltpu.CompilerParams(dimension_semantics=sem, vmem_limit_bytes=VMEM_LIMIT_BYTES)


def _rms(x, g):
    ms = jnp.mean(x * x, axis=-1, keepdims=True)
    return x * lax.rsqrt(ms + EPS) * g


def _silu(x):
    return x * jax.nn.sigmoid(x)


def _dot_nt(a, b):
    return lax.dot_general(a, b, (((1,), (1,)), ((), ())), preferred_element_type=F32)


def _dot_tn(a, b):
    return lax.dot_general(a, b, (((0,), (0,)), ((), ())), preferred_element_type=F32)


def _ffn_body(*refs, emit_next):
    if emit_next:
        x_ref, pre_ref, post_ref, nxt_ref, wg_ref, wu_ref, wd_ref, out_ref, u_ref, xn_s, acc_s = refs
    else:
        x_ref, pre_ref, post_ref, wg_ref, wu_ref, wd_ref, out_ref, xn_s, acc_s = refs
    j = pl.program_id(1)

    @pl.when(j == 0)
    def _():
        xn_s[...] = _rms(x_ref[...], pre_ref[...]).astype(BF16)

    xn = xn_s[...]
    g = jnp.dot(xn, wg_ref[...], preferred_element_type=F32)
    u = jnp.dot(xn, wu_ref[...], preferred_element_type=F32)
    h = (_silu(g) * u).astype(BF16)
    part = jnp.dot(h, wd_ref[...], preferred_element_type=F32)

    @pl.when(j == 0)
    def _():
        acc_s[...] = part

    @pl.when(j > 0)
    def _():
        acc_s[...] += part

    @pl.when(j == pl.num_programs(1) - 1)
    def _():
        x1 = x_ref[...] + 0.5 * _rms(acc_s[...], post_ref[...])
        out_ref[...] = x1
        if emit_next:
            u_ref[...] = _rms(x1, nxt_ref[...]).astype(BF16)


def _ffn(x, pre, post, nxt, wg, wu, wd, tm, tf):
    n, d = x.shape
    fp = wg.shape[1]
    emit_next = nxt is not None
    row = lambda i, j: (i, 0)
    vec = pl.BlockSpec((1, d), lambda i, j: (0, 0))
    in_specs = [pl.BlockSpec((tm, d), row), vec, vec]
    args = [x, pre, post]
    if emit_next:
        in_specs.append(vec)
        args.append(nxt)
    in_specs += [pl.BlockSpec((d, tf), lambda i, j: (0, j)),
                 pl.BlockSpec((d, tf), lambda i, j: (0, j)),
                 pl.BlockSpec((tf, d), lambda i, j: (j, 0))]
    args += [wg, wu, wd]
    out_shape = [jax.ShapeDtypeStruct((n, d), F32)]
    out_specs = [pl.BlockSpec((tm, d), row)]
    if emit_next:
        out_shape.append(jax.ShapeDtypeStruct((n, d), BF16))
        out_specs.append(pl.BlockSpec((tm, d), row))
    res = pl.pallas_call(
        functools.partial(_ffn_body, emit_next=emit_next),
        grid=(n // tm, fp // tf),
        in_specs=in_specs, out_specs=out_specs, out_shape=out_shape,
        scratch_shapes=[pltpu.VMEM((tm, d), BF16), pltpu.VMEM((tm, d), F32)],
        compiler_params=_params(("parallel", "arbitrary")),
        name="ffn_next" if emit_next else "ffn",
    )(*args)
    return res if emit_next else res[0]


def _proj_body(u_ref, w_ref, s_ref, out_ref, ff_ref, *, n_linear, ff_tile, ff_off):
    j = pl.program_id(1)
    acc = jnp.dot(u_ref[...], w_ref[...], preferred_element_type=F32)

    @pl.when(j < n_linear)
    def _():
        out_ref[...] = (acc * s_ref[...]).astype(out_ref.dtype)

    @pl.when(j == ff_tile)
    def _():
        ff_ref[...] = acc[:, ff_off:ff_off + LANES]

    @pl.when(j >= n_linear)
    def _():
        out_ref[...] = jax.nn.sigmoid(acc).astype(out_ref.dtype)


def _in_proj(u, wbig, scale, n_linear_cols, ff_col, tm, tn):
    n, d = u.shape
    cols = wbig.shape[1]
    body = functools.partial(_proj_body, n_linear=n_linear_cols // tn,
                             ff_tile=ff_col // tn, ff_off=ff_col % tn)
    return pl.pallas_call(
        body,
        grid=(n // tm, cols // tn),
        in_specs=[pl.BlockSpec((tm, d), lambda i, j: (i, 0)),
                  pl.BlockSpec((d, tn), lambda i, j: (0, j)),
                  pl.BlockSpec((1, tn), lambda i, j: (0, j))],
        out_specs=[pl.BlockSpec((tm, tn), lambda i, j: (i, j)),
                   pl.BlockSpec((tm, LANES), lambda i, j: (i, 0))],
        out_shape=[jax.ShapeDtypeStruct((n, cols), BF16),
                   jax.ShapeDtypeStruct((n, LANES), F32)],
        compiler_params=_params(("parallel", "arbitrary")),
        name="in_proj",
    )(u, wbig, scale)


def _hgrn_levels(chunk):
    return [chunk >> (i + 1) for i in range(int(math.log2(chunk)))]


def _hgrn_constants(chunk):
    t = np.arange(chunk)[:, None]
    r = np.arange(chunk)[None, :]
    blocks = [(r <= t), (r > t)]
    masks = [(r == t)]
    for h in _hgrn_levels(chunk):
        ref = (t // (2 * h)) * (2 * h) + h - 1
        upper = (t & h) != 0
        blocks.append(np.where(upper, (r > ref) & (r <= t), (r > t) & (r <= ref)))
        masks.append(((t ^ r) >= h) & ((t ^ r) < 2 * h) & upper)
    a = np.concatenate(blocks, axis=0).astype(np.float32)
    m = np.stack(masks, axis=0).astype(np.float32)
    return jnp.asarray(a, BF16), jnp.asarray(m, F32)


def _hgrn_body(lb_ref, gn_ref, a_ref, mask_ref, hq_ref, hf_ref, hi_ref, hog_ref, o_ref, st_s,
               *, chunk, n_chunks, layer):
    @pl.when(pl.program_id(2) == 0)
    def _():
        st_s[...] = jnp.zeros_like(st_s)

    slots = [lb_ref[r:r + 1, :] for r in range(lb_ref.shape[0])]
    top = functools.reduce(jnp.maximum, slots)
    es = [jnp.exp(s - top) for s in slots]
    lb = sum(es[1:layer + 1], es[0]) / sum(es[1:], es[0])
    gn = gn_ref[...]
    levels = _hgrn_levels(chunk)
    row = lax.broadcasted_iota(jnp.int32, (chunk, HEAD_DIM), 0)

    def one_chunk(c, carry):
        sl = pl.ds(pl.multiple_of(c * chunk, chunk), chunk)
        hq = hq_ref[sl, :].astype(F32)
        hf = hf_ref[sl, :].astype(F32)
        v = hi_ref[sl, :]
        hog = hog_ref[sl, :].astype(F32)
        g = lb + (1.0 - lb) * jax.nn.sigmoid(hf)
        lg = jnp.log(g)
        kk = 1.0 - g
        q = _silu(hq)
        lg_hi = lg.astype(BF16)
        lg_lo = (lg - lg_hi.astype(F32)).astype(BF16)
        ee = jnp.dot(a_ref[...], jnp.concatenate([lg_hi, lg_lo], axis=1), preferred_element_type=F32)
        ex = jnp.exp(ee[:, :HEAD_DIM] + ee[:, HEAD_DIM:])
        b_exp = ex[0:chunk]
        to_end = ex[chunk:2 * chunk]
        st = st_s[...]
        o = _dot_nt((q * b_exp).astype(BF16), st.astype(BF16))
        scores = _dot_nt(q.astype(BF16), kk.astype(BF16)) * mask_ref[0]
        for li, h in enumerate(levels):
            f = ex[(2 + li) * chunk:(3 + li) * chunk]
            y = (jnp.where((row & h) != 0, q, kk) * f).astype(BF16)
            scores = scores + _dot_nt(y, y) * mask_ref[1 + li]
        o = o + jnp.dot(scores.astype(BF16), v, preferred_element_type=F32)
        st_s[...] = st * ex[chunk - 1:chunk] + _dot_tn(v, (kk * to_end).astype(BF16))
        o = _rms(o, gn) * _silu(hog)
        o_ref[sl, :] = o.astype(o_ref.dtype)
        return carry

    lax.fori_loop(0, n_chunks, one_chunk, 0)


def _hgrn(proj, hgrn_lb, layer, gnorm, n_heads, col0, chunk, tt):
    bsz, seq, _ = proj.shape
    a_mat, masks = _hgrn_constants(chunk)
    n_slots = hgrn_lb.shape[0]

    def col(k):
        return pl.BlockSpec((None, tt, HEAD_DIM), lambda b, h, t, k=k: (b, t, col0 + k * n_heads + h))

    return pl.pallas_call(
        functools.partial(_hgrn_body, chunk=chunk, n_chunks=tt // chunk, layer=layer),
        grid=(bsz, n_heads, seq // tt),
        in_specs=[pl.BlockSpec((n_slots, HEAD_DIM), lambda b, h, t: (0, h)),
                  pl.BlockSpec((1, HEAD_DIM), lambda b, h, t: (0, h)),
                  pl.BlockSpec(a_mat.shape, lambda b, h, t: (0, 0)),
                  pl.BlockSpec(masks.shape, lambda b, h, t: (0, 0, 0)),
                  col(0), col(1), col(2), col(3)],
        out_specs=pl.BlockSpec((None, tt, HEAD_DIM), lambda b, h, t: (b, t, h)),
        out_shape=jax.ShapeDtypeStruct((bsz, seq, n_heads * HEAD_DIM), BF16),
        scratch_shapes=[pltpu.VMEM((HEAD_DIM, HEAD_DIM), F32)],
        compiler_params=_params(("parallel", "parallel", "arbitrary")),
        name="hgrn",
    )(hgrn_lb, gnorm, a_mat, masks, proj, proj, proj, proj)


def _split3(x):
    hi = x.astype(BF16).astype(F32)
    r = x - hi
    mid = r.astype(BF16).astype(F32)
    lo = (r - mid).astype(BF16).astype(F32)
    return hi, mid, lo


def _fox_prep_body(ff_ref, fb_ref, tri_ref, qa_ref, ka_ref, carry_s, *, n_heads):
    @pl.when(pl.program_id(1) == 0)
    def _():
        carry_s[...] = jnp.zeros_like(carry_s)

    tt = ff_ref.shape[0]
    z = ff_ref[...] + fb_ref[...]
    log_f = jnp.minimum(z, 0.0) - jnp.log(1.0 + jnp.exp(-jnp.abs(z)))
    tri = tri_ref[...]
    c = carry_s[...]
    for piece in _split3(log_f):
        c = c + jnp.dot(tri, piece.astype(BF16), preferred_element_type=F32)
    carry_s[...] = c[tt - 1:tt, :]
    c = c * LOG2E
    lane = lax.broadcasted_iota(jnp.int32, (tt, LANES), 1)
    for h in range(n_heads):
        hi, mid, lo = _split3(jnp.broadcast_to(c[:, h:h + 1], (tt, LANES)))
        qa = jnp.where(lane == 0, hi, jnp.where(lane == 1, mid, jnp.where(lane == 2, lo,
             jnp.where(lane < 6, 1.0, 0.0))))
        ka = jnp.where(lane < 3, 1.0, jnp.where(lane == 3, -hi, jnp.where(lane == 4, -mid,
             jnp.where(lane == 5, -lo, 0.0))))
        qa_ref[:, h * LANES:(h + 1) * LANES] = qa.astype(BF16)
        ka_ref[:, h * LANES:(h + 1) * LANES] = ka.astype(BF16)


def _fox_prep(ff, fox_b, n_heads, tt):
    bsz, seq, _ = ff.shape
    tri = jnp.asarray(np.tril(np.ones((tt, tt), np.float32)), BF16)
    out = jax.ShapeDtypeStruct((bsz, seq, n_heads * LANES), BF16)
    blk = pl.BlockSpec((None, tt, n_heads * LANES), lambda b, t: (b, t, 0))
    return pl.pallas_call(
        functools.partial(_fox_prep_body, n_heads=n_heads),
        grid=(bsz, seq // tt),
        in_specs=[pl.BlockSpec((None, tt, LANES), lambda b, t: (b, t, 0)),
                  pl.BlockSpec((1, LANES), lambda b, t: (0, 0)),
                  pl.BlockSpec((tt, tt), lambda b, t: (0, 0))],
        out_specs=[blk, blk],
        out_shape=[out, out],
        scratch_shapes=[pltpu.VMEM((1, LANES), F32)],
        compiler_params=_params(("parallel", "arbitrary")),
        name="fox_prep",
    )(ff, fox_b, tri)


def _fox_body(q_ref, qa_ref, k_ref, ka_ref, v_ref, o_ref, *, tile):
    qi = pl.program_id(2)
    qc = jnp.concatenate([q_ref[...], qa_ref[...]], axis=1)

    def block(j, carry, diagonal):
        m, l, acc = carry
        ks = pl.ds(pl.multiple_of(j * tile, tile), tile)
        kc = jnp.concatenate([k_ref[ks, :], ka_ref[ks, :]], axis=1)
        s = _dot_nt(qc, kc)
        if diagonal:
            r = lax.broadcasted_iota(jnp.int32, (tile, tile), 0)
            c = lax.broadcasted_iota(jnp.int32, (tile, tile), 1)
            s = jnp.where(c <= r, s, NEG_BIG)
        m_new = jnp.maximum(m, jnp.max(s, axis=1, keepdims=True))
        alpha = jnp.exp2(m - m_new)
        p = jnp.exp2(s - m_new)
        l = alpha * l + jnp.sum(p, axis=1, keepdims=True)
        acc = alpha * acc + jnp.dot(p.astype(BF16), v_ref[ks, :], preferred_element_type=F32)
        return m_new, l, acc

    init = (jnp.full((tile, 1), NEG_BIG, F32), jnp.zeros((tile, 1), F32),
            jnp.zeros((tile, HEAD_DIM), F32))
    carry = block(qi, init, True)
    m, l, acc = lax.fori_loop(0, qi, lambda j, cr: block(j, cr, False), carry)
    o_ref[...] = (acc / l).astype(o_ref.dtype)


def _fox(proj, qa, ka, n_heads, col_q, col_k, col_v, tile):
    bsz, seq, _ = proj.shape
    qblk = lambda c0: pl.BlockSpec((None, tile, HEAD_DIM), lambda b, h, i: (b, i, c0 + h))
    kblk = lambda c0: pl.BlockSpec((None, seq, HEAD_DIM), lambda b, h, i: (b, 0, c0 + h))
    return pl.pallas_call(
        functools.partial(_fox_body, tile=tile),
        grid=(bsz, n_heads, seq // tile),
        in_specs=[qblk(col_q), qblk(0), kblk(col_k), kblk(0), kblk(col_v)],
        out_specs=pl.BlockSpec((None, tile, HEAD_DIM), lambda b, h, i: (b, i, h)),
        out_shape=jax.ShapeDtypeStruct((bsz, seq, n_heads * HEAD_DIM), BF16),
        compiler_params=_params(("parallel", "parallel", "arbitrary")),
        name="fox",
    )(proj, qa, proj, ka, proj)


def _mem_kv_body(mem_ref, g_ref, w_ref, out_ref):
    mn = _rms(mem_ref[...], g_ref[...]).astype(BF16)
    out_ref[...] = jnp.dot(mn, w_ref[...], preferred_element_type=F32).astype(out_ref.dtype)


def _mem_kv(mem, g, w):
    bsz, m, d = mem.shape
    cols = w.shape[1]
    return pl.pallas_call(
        _mem_kv_body,
        grid=(bsz,),
        in_specs=[pl.BlockSpec((None, m, d), lambda b: (b, 0, 0)),
                  pl.BlockSpec((1, d), lambda b: (0, 0)),
                  pl.BlockSpec((d, cols), lambda b: (0, 0))],
        out_specs=pl.BlockSpec((None, m, cols), lambda b: (b, 0, 0)),
        out_shape=jax.ShapeDtypeStruct((bsz, m, cols), BF16),
        compiler_params=_params(("parallel",)),
        name="mem_kv",
    )(mem, g, w)


def _mem_attn_body(q_ref, kv_ref, o_ref, *, n_heads):
    width = n_heads * HEAD_DIM
    scale = HEAD_DIM ** -0.5
    for h in range(n_heads):
        lo, hi = h * HEAD_DIM, (h + 1) * HEAD_DIM
        s = _dot_nt(q_ref[:, lo:hi], kv_ref[:, lo:hi]) * scale
        p = jnp.exp(s - jnp.max(s, axis=1, keepdims=True))
        p = p / jnp.sum(p, axis=1, keepdims=True)
        o = jnp.dot(p.astype(BF16), kv_ref[:, width + lo:width + hi], preferred_element_type=F32)
        o_ref[:, lo:hi] = o.astype(o_ref.dtype)


def _mem_attn(proj, mem_kv, n_heads, col_q, tq):
    bsz, seq, _ = proj.shape
    width = n_heads * HEAD_DIM
    m = mem_kv.shape[1]
    return pl.pallas_call(
        functools.partial(_mem_attn_body, n_heads=n_heads),
        grid=(bsz, seq // tq),
        in_specs=[pl.BlockSpec((None, tq, width), lambda b, i: (b, i, col_q)),
                  pl.BlockSpec((None, m, 2 * width), lambda b, i: (b, 0, 0))],
        out_specs=pl.BlockSpec((None, tq, width), lambda b, i: (b, i, 0)),
        out_shape=jax.ShapeDtypeStruct((bsz, seq, width), BF16),
        compiler_params=_params(("parallel", "parallel")),
        name="mem_attn",
    )(proj, mem_kv)


def _merge_body(oh_ref, of_ref, om_ref, g0_ref, g1_ref, g2_ref, x_ref, post_ref,
                wh_ref, wf_ref, wm_ref, wo_ref, out_ref):
    merged = g0_ref[...].astype(F32) * jnp.dot(oh_ref[...], wh_ref[...], preferred_element_type=F32)
    merged += g1_ref[...].astype(F32) * jnp.dot(of_ref[...], wf_ref[...], preferred_element_type=F32)
    merged += g2_ref[...].astype(F32) * jnp.dot(om_ref[...], wm_ref[...], preferred_element_type=F32)
    y = jnp.dot(merged.astype(BF16), wo_ref[...], preferred_element_type=F32)
    out_ref[...] = x_ref[...] + _rms(y, post_ref[...])


def _merge(o_h, o_f, o_m, proj, gate_col, x1, post, wh, wf, wm, wo, tm):
    n, d = x1.shape
    row = lambda width: pl.BlockSpec((tm, width), lambda i: (i, 0))
    gate = lambda k: pl.BlockSpec((tm, d), lambda i, k=k: (i, gate_col + k))
    full = lambda w: pl.BlockSpec(w.shape, lambda i: (0, 0), pipeline_mode=pl.Buffered(1))
    return pl.pallas_call(
        _merge_body,
        grid=(n // tm,),
        in_specs=[row(o_h.shape[1]), row(o_f.shape[1]), row(o_m.shape[1]),
                  gate(0), gate(1), gate(2), row(d),
                  pl.BlockSpec((1, d), lambda i: (0, 0)),
                  full(wh), full(wf), full(wm), full(wo)],
        out_specs=row(d),
        out_shape=jax.ShapeDtypeStruct((n, d), F32),
        compiler_params=_params(("parallel",)),
        name="merge",
    )(o_h, o_f, o_m, proj, proj, proj, x1, post, wh, wf, wm, wo)


def _pad_cols(w, to):
    return jnp.pad(w, ((0, 0), (0, to - w.shape[1])))


def _ffn_weights(wg, wu, wd, tf):
    f = wg.shape[1]
    fp = -(-f // tf) * tf
    return (_pad_cols(wg, fp).astype(BF16), _pad_cols(wu, fp).astype(BF16),
            jnp.pad(wd, ((0, fp - f), (0, 0))).astype(BF16))


def kernel(x, mem, ffn1_pre, ffn1_post, ffn1_wg, ffn1_wu, ffn1_wd, mix_pre, mix_post, mem_norm,
           w_in, hgrn_lb, hgrn_gnorm, fox_fb, w_mem_kv, w_hgrn_out, w_fox_out, w_mem_out, w_gate,
           w_o, ffn2_pre, ffn2_post, ffn2_wg, ffn2_wu, ffn2_wd):
    bsz, seq, d = x.shape
    n = bsz * seq
    depth = ffn1_pre.shape[0]
    wh, wf, wm = w_hgrn_out.shape[1], w_fox_out.shape[1], w_mem_out.shape[1]
    h_hgrn, h_fox, h_mem = wh // HEAD_DIM, wf // HEAD_DIM, wm // HEAD_DIM
    c_fox = 4 * wh
    c_ff = c_fox + 3 * wf
    ff_pad = 2 * LANES
    c_mq = c_ff + ff_pad
    c_gate = c_mq + wm
    assert h_fox <= LANES and c_mq % wm == 0 and c_gate % d == 0
    t = _tiles(n, seq, c_gate, w_gate.shape[2], c_ff)
    scale = np.ones((1, c_gate + w_gate.shape[2]), np.float32)
    scale[:, c_fox:c_fox + wf] = HEAD_DIM ** -0.5 * LOG2E
    scale = jnp.asarray(scale)

    x2 = x.reshape(n, d)
    vec = lambda a: a.reshape(1, -1)
    for l in range(depth):
        w_in_l = w_in[l]
        wbig = jnp.concatenate(
            [w_in_l[:, :c_ff], _pad_cols(w_in_l[:, c_ff:c_ff + h_fox], ff_pad),
             w_in_l[:, c_ff + h_fox:], w_gate[l]], axis=1).astype(BF16)

        wg1, wu1, wd1 = _ffn_weights(ffn1_wg[l], ffn1_wu[l], ffn1_wd[l], t["ffn_tf"])
        x1, u = _ffn(x2, vec(ffn1_pre[l]), vec(ffn1_post[l]), vec(mix_pre[l]), wg1, wu1, wd1,
                     t["ffn_tm"], t["ffn_tf"])

        proj, ff = _in_proj(u, wbig, scale, c_gate, c_ff, t["proj_tm"], t["proj_tn"])
        proj3 = proj.reshape(bsz, seq, -1)

        o_h = _hgrn(proj3, hgrn_lb, l, vec(hgrn_gnorm[l]), h_hgrn, 0, t["hgrn_chunk"], t["hgrn_tt"])

        qa, ka = _fox_prep(ff.reshape(bsz, seq, LANES), _pad_cols(vec(fox_fb[l]), LANES), h_fox,
                           t["prep_tt"])
        blk = lambda c: c // HEAD_DIM
        o_f = _fox(proj3, qa, ka, h_fox, blk(c_fox), blk(c_fox + wf), blk(c_fox + 2 * wf), t["fox_t"])

        mkv = _mem_kv(mem, vec(mem_norm[l]), w_mem_kv[l].astype(BF16))
        o_m = _mem_attn(proj3, mkv, h_mem, c_mq // wm, t["mem_tq"])

        x2 = _merge(o_h.reshape(n, wh), o_f.reshape(n, wf), o_m.reshape(n, wm), proj, c_gate // d,
                    x1, vec(mix_post[l]), w_hgrn_out[l].astype(BF16), w_fox_out[l].astype(BF16),
                    w_mem_out[l].astype(BF16), w_o[l].astype(BF16), t["merge_tm"])

        wg2, wu2, wd2 = _ffn_weights(ffn2_wg[l], ffn2_wu[l], ffn2_wd[l], t["ffn_tf"])
        x2 = _ffn(x2, vec(ffn2_pre[l]), vec(ffn2_post[l]), None, wg2, wu2, wd2,
                  t["ffn_tm"], t["ffn_tf"])
    return x2.reshape(bsz, seq, d)
```

```python
import functools
import math

import numpy as np
import jax
import jax.numpy as jnp
from jax import lax
from jax.experimental import pallas as pl
from jax.experimental.pallas import tpu as pltpu

F32 = jnp.float32
BF16 = jnp.bfloat16
EPS = 1e-6
HEAD_DIM = 128
LANES = 128
LOG2E = math.log2(math.e)
VMEM_LIMIT_BYTES = 56 * 1024 * 1024
NEG_BIG = -1e30


def _tiles(n_tokens, seq, proj_cols, gate_cols, fox_heads, fox_head_cols):
    widest = lambda cols: next(w for w in (1024, 768, 512, 256, 128) if cols % w == 0)
    fox_group = next(g for g in (3, 2, 1)
                     if fox_heads % g == 0 and all(c % g == 0 for c in fox_head_cols))
    return dict(
        ffn_tm=min(512, n_tokens), ffn_tf=512,
        proj_tm=min(1024, n_tokens), proj_tn=widest(proj_cols), gate_tn=widest(gate_cols),
        hgrn_chunk=min(128, seq), hgrn_tt=min(512, seq),
        prep_tt=min(256, seq),
        fox_t=min(512, seq), fox_group=fox_group,
        mem_tq=min(512, seq),
        merge_tm=min(256, n_tokens),
    )


def _params(sem):
    return pltpu.CompilerParams(dimension_semantics=sem, vmem_limit_bytes=VMEM_LIMIT_BYTES)


def _rms(x, g):
    ms = jnp.mean(x * x, axis=-1, keepdims=True)
    return x * lax.rsqrt(ms + EPS) * g


def _silu(x):
    return x * jax.nn.sigmoid(x)


def _dot_nt(a, b):
    return lax.dot_general(a, b, (((1,), (1,)), ((), ())), preferred_element_type=F32)


def _dot_tn(a, b):
    return lax.dot_general(a, b, (((0,), (0,)), ((), ())), preferred_element_type=F32)


def _ffn_body(*refs, emit_next, last_width):
    if emit_next:
        x_ref, pre_ref, post_ref, nxt_ref, wg_ref, wu_ref, wd_ref, out_ref, u_ref, xn_s, acc_s = refs
    else:
        x_ref, pre_ref, post_ref, wg_ref, wu_ref, wd_ref, out_ref, xn_s, acc_s = refs
    j = pl.program_id(1)
    last = pl.num_programs(1) - 1

    @pl.when(j == 0)
    def _():
        xn_s[...] = _rms(x_ref[...], pre_ref[...]).astype(BF16)
        acc_s[...] = jnp.zeros_like(acc_s)

    def add_hidden_tile(width):
        xn = xn_s[...]
        g = jnp.dot(xn, wg_ref[:, :width], preferred_element_type=F32)
        u = jnp.dot(xn, wu_ref[:, :width], preferred_element_type=F32)
        h = (_silu(g) * u).astype(BF16)
        acc_s[...] += jnp.dot(h, wd_ref[:width, :], preferred_element_type=F32)

    @pl.when(j < last)
    def _():
        add_hidden_tile(wg_ref.shape[1])

    @pl.when(j == last)
    def _():
        add_hidden_tile(last_width)
        x1 = x_ref[...] + 0.5 * _rms(acc_s[...], post_ref[...])
        out_ref[...] = x1
        if emit_next:
            u_ref[...] = _rms(x1, nxt_ref[...]).astype(BF16)


def _ffn(x, pre, post, nxt, wg, wu, wd, tm, tf):
    n, d = x.shape
    f = wg.shape[1]
    n_f = pl.cdiv(f, tf)
    emit_next = nxt is not None
    row = lambda i, j: (i, 0)
    vec = pl.BlockSpec((1, d), lambda i, j: (0, 0))
    in_specs = [pl.BlockSpec((tm, d), row), vec, vec]
    args = [x, pre, post]
    if emit_next:
        in_specs.append(vec)
        args.append(nxt)
    in_specs += [pl.BlockSpec((d, tf), lambda i, j: (0, j)),
                 pl.BlockSpec((d, tf), lambda i, j: (0, j)),
                 pl.BlockSpec((tf, d), lambda i, j: (j, 0))]
    args += [wg, wu, wd]
    out_shape = [jax.ShapeDtypeStruct((n, d), F32)]
    out_specs = [pl.BlockSpec((tm, d), row)]
    if emit_next:
        out_shape.append(jax.ShapeDtypeStruct((n, d), BF16))
        out_specs.append(pl.BlockSpec((tm, d), row))
    res = pl.pallas_call(
        functools.partial(_ffn_body, emit_next=emit_next, last_width=f - (n_f - 1) * tf),
        grid=(n // tm, n_f),
        in_specs=in_specs, out_specs=out_specs, out_shape=out_shape,
        scratch_shapes=[pltpu.VMEM((tm, d), BF16), pltpu.VMEM((tm, d), F32)],
        compiler_params=_params(("parallel", "arbitrary")),
        name="ffn_next" if emit_next else "ffn",
    )(*args)
    return res if emit_next else res[0]


def _proj_body(u_ref, w_ref, s_ref, out_ref, wb_s, *, gate):
    @pl.when(pl.program_id(1) == 0)
    def _():
        wb_s[...] = w_ref[...].astype(BF16)

    acc = jnp.dot(u_ref[...], wb_s[...], preferred_element_type=F32)
    if gate:
        out_ref[...] = jax.nn.sigmoid(acc).astype(out_ref.dtype)
    else:
        out_ref[...] = (acc * s_ref[...]).astype(out_ref.dtype)


def _proj(u, w, layer, cols, scale, gate, tm, tn, name):
    n, d = u.shape
    assert cols % tn == 0
    return pl.pallas_call(
        functools.partial(_proj_body, gate=gate),
        grid=(cols // tn, n // tm),
        in_specs=[pl.BlockSpec((tm, d), lambda j, i: (i, 0)),
                  pl.BlockSpec((None, d, tn), lambda j, i: (layer, 0, j)),
                  pl.BlockSpec((1, tn), lambda j, i: (0, j))],
        out_specs=pl.BlockSpec((tm, tn), lambda j, i: (i, j)),
        out_shape=jax.ShapeDtypeStruct((n, cols), BF16),
        scratch_shapes=[pltpu.VMEM((d, tn), BF16)],
        compiler_params=_params(("parallel", "arbitrary")),
        name=name,
    )(u, w, scale)


def _hgrn_levels(chunk):
    return [chunk >> (i + 1) for i in range(int(math.log2(chunk)))]


def _hgrn_constants(chunk):
    t = np.arange(chunk)[:, None]
    r = np.arange(chunk)[None, :]
    blocks = [(r <= t), (r > t)]
    masks = [(r == t)]
    for h in _hgrn_levels(chunk):
        ref = (t // (2 * h)) * (2 * h) + h - 1
        upper = (t & h) != 0
        blocks.append(np.where(upper, (r > ref) & (r <= t), (r > t) & (r <= ref)))
        masks.append(((t ^ r) >= h) & ((t ^ r) < 2 * h) & upper)
    a = np.concatenate(blocks, axis=0).astype(np.float32)
    m = np.stack(masks, axis=0).astype(np.float32)
    return jnp.asarray(a, BF16), jnp.asarray(m, F32)


def _hgrn_body(lb_ref, gn_ref, a_ref, mask_ref, hq_ref, hf_ref, hi_ref, hog_ref, o_ref, st_s,
               *, chunk, n_chunks, layer):
    @pl.when(pl.program_id(2) == 0)
    def _():
        st_s[...] = jnp.zeros_like(st_s)

    slots = [lb_ref[r:r + 1, :] for r in range(lb_ref.shape[0])]
    top = functools.reduce(jnp.maximum, slots)
    es = [jnp.exp(s - top) for s in slots]
    lb = sum(es[1:layer + 1], es[0]) / sum(es[1:], es[0])
    gn = gn_ref[...]
    levels = _hgrn_levels(chunk)
    row = lax.broadcasted_iota(jnp.int32, (chunk, HEAD_DIM), 0)

    def one_chunk(c, carry):
        sl = pl.ds(pl.multiple_of(c * chunk, chunk), chunk)
        hq = hq_ref[sl, :].astype(F32)
        hf = hf_ref[sl, :].astype(F32)
        v = hi_ref[sl, :]
        hog = hog_ref[sl, :].astype(F32)
        g = lb + (1.0 - lb) * jax.nn.sigmoid(hf)
        lg = jnp.log(g)
        kk = 1.0 - g
        q = _silu(hq)
        lg_hi = lg.astype(BF16)
        lg_lo = (lg - lg_hi.astype(F32)).astype(BF16)
        ee = jnp.dot(a_ref[...], jnp.concatenate([lg_hi, lg_lo], axis=1), preferred_element_type=F32)
        ex = jnp.exp(ee[:, :HEAD_DIM] + ee[:, HEAD_DIM:])
        b_exp = ex[0:chunk]
        to_end = ex[chunk:2 * chunk]
        st = st_s[...]
        o = _dot_nt((q * b_exp).astype(BF16), st.astype(BF16))
        scores = _dot_nt(q.astype(BF16), kk.astype(BF16)) * mask_ref[0]
        for li, h in enumerate(levels):
            f = ex[(2 + li) * chunk:(3 + li) * chunk]
            y = (jnp.where((row & h) != 0, q, kk) * f).astype(BF16)
            scores = scores + _dot_nt(y, y) * mask_ref[1 + li]
        o = o + jnp.dot(scores.astype(BF16), v, preferred_element_type=F32)
        st_s[...] = st * ex[chunk - 1:chunk] + _dot_tn(v, (kk * to_end).astype(BF16))
        o = _rms(o, gn) * _silu(hog)
        o_ref[sl, :] = o.astype(o_ref.dtype)
        return carry

    lax.fori_loop(0, n_chunks, one_chunk, 0, unroll=True)


def _hgrn(proj, hgrn_lb, layer, gnorm, n_heads, col0, chunk, tt):
    bsz, seq, _ = proj.shape
    a_mat, masks = _hgrn_constants(chunk)
    n_slots = hgrn_lb.shape[0]

    def col(k):
        return pl.BlockSpec((None, tt, HEAD_DIM), lambda b, h, t, k=k: (b, t, col0 + k * n_heads + h))

    return pl.pallas_call(
        functools.partial(_hgrn_body, chunk=chunk, n_chunks=tt // chunk, layer=layer),
        grid=(bsz, n_heads, seq // tt),
        in_specs=[pl.BlockSpec((n_slots, HEAD_DIM), lambda b, h, t: (0, h)),
                  pl.BlockSpec((1, HEAD_DIM), lambda b, h, t: (0, h)),
                  pl.BlockSpec(a_mat.shape, lambda b, h, t: (0, 0)),
                  pl.BlockSpec(masks.shape, lambda b, h, t: (0, 0, 0)),
                  col(0), col(1), col(2), col(3)],
        out_specs=pl.BlockSpec((None, tt, HEAD_DIM), lambda b, h, t: (b, t, h)),
        out_shape=jax.ShapeDtypeStruct((bsz, seq, n_heads * HEAD_DIM), BF16),
        scratch_shapes=[pltpu.VMEM((HEAD_DIM, HEAD_DIM), F32)],
        compiler_params=_params(("parallel", "parallel", "arbitrary")),
        name="hgrn",
    )(hgrn_lb, gnorm, a_mat, masks, proj, proj, proj, proj)


def _split3(x):
    hi = x.astype(BF16).astype(F32)
    r = x - hi
    mid = r.astype(BF16).astype(F32)
    lo = (r - mid).astype(BF16).astype(F32)
    return hi, mid, lo


V_ROWS = HEAD_DIM + 16


def _fox_prep_body(u_ref, wff_ref, fb_ref, tri_ref, v_ref, qa_ref, ka_ref, vt_ref, carry_s,
                   *, n_heads):
    @pl.when(pl.program_id(1) == 0)
    def _():
        carry_s[...] = jnp.zeros_like(carry_s)

    tt = u_ref.shape[0]
    z = jnp.dot(u_ref[...], wff_ref[...], preferred_element_type=F32) + fb_ref[...]
    log_f = jnp.minimum(z, 0.0) - jnp.log(1.0 + jnp.exp(-jnp.abs(z)))
    tri = tri_ref[...]
    c = carry_s[...]
    for piece in _split3(log_f):
        c = c + jnp.dot(tri, piece.astype(BF16), preferred_element_type=F32)
    carry_s[...] = c[tt - 1:tt, :]
    c = c * LOG2E
    lane = lax.broadcasted_iota(jnp.int32, (tt, LANES), 1)
    ones = jnp.ones((V_ROWS - HEAD_DIM, tt), BF16)
    for h in range(n_heads):
        ln = slice(h * LANES, (h + 1) * LANES)
        hi, mid, lo = _split3(jnp.broadcast_to(c[:, h:h + 1], (tt, LANES)))
        qa = jnp.where(lane == 0, hi, jnp.where(lane == 1, mid, jnp.where(lane == 2, lo,
             jnp.where(lane < 6, 1.0, 0.0))))
        ka = jnp.where(lane < 3, 1.0, jnp.where(lane == 3, -hi, jnp.where(lane == 4, -mid,
             jnp.where(lane == 5, -lo, 0.0))))
        qa_ref[:, ln] = qa.astype(BF16)
        ka_ref[:, ln] = ka.astype(BF16)
        vt_ref[h * V_ROWS:h * V_ROWS + HEAD_DIM, :] = v_ref[:, ln].astype(F32).T.astype(BF16)
        vt_ref[h * V_ROWS + HEAD_DIM:(h + 1) * V_ROWS, :] = ones


def _fox_prep(u, w_ff, fox_b, proj, col_v, n_heads, tt):
    bsz, seq, d = u.shape
    width = n_heads * LANES
    assert col_v % n_heads == 0
    tri = jnp.asarray(np.tril(np.ones((tt, tt), np.float32)), BF16)
    out = jax.ShapeDtypeStruct((bsz, seq, width), BF16)
    blk = pl.BlockSpec((None, tt, width), lambda b, t: (b, t, 0))
    return pl.pallas_call(
        functools.partial(_fox_prep_body, n_heads=n_heads),
        grid=(bsz, seq // tt),
        in_specs=[pl.BlockSpec((None, tt, d), lambda b, t: (b, t, 0)),
                  pl.BlockSpec((d, LANES), lambda b, t: (0, 0)),
                  pl.BlockSpec((1, LANES), lambda b, t: (0, 0)),
                  pl.BlockSpec((tt, tt), lambda b, t: (0, 0)),
                  pl.BlockSpec((None, tt, width), lambda b, t: (b, t, col_v // n_heads))],
        out_specs=[blk, blk, pl.BlockSpec((None, n_heads * V_ROWS, tt), lambda b, t: (b, 0, t))],
        out_shape=[out, out, jax.ShapeDtypeStruct((bsz, n_heads * V_ROWS, seq), BF16)],
        scratch_shapes=[pltpu.VMEM((1, LANES), F32)],
        compiler_params=_params(("parallel", "arbitrary")),
        name="fox_prep",
    )(u, w_ff, fox_b, tri, proj)


def _fox_body(q_ref, qa_ref, k_ref, ka_ref, vt_ref, o_ref, acc_s, st_s, *, tile, group):
    qi = pl.program_id(2)
    lanes = [slice(g * HEAD_DIM, (g + 1) * HEAD_DIM) for g in range(group)]
    qcs = [jnp.concatenate([q_ref[:, ln], qa_ref[:, ln]], axis=1) for ln in lanes]

    def logits(g, j):
        ks = pl.ds(pl.multiple_of(j * tile, tile), tile)
        kc = jnp.concatenate([k_ref[ks, lanes[g]], ka_ref[ks, lanes[g]]], axis=1)
        return _dot_nt(kc, qcs[g])

    def accumulate(g, j, m, alpha):
        ks = pl.ds(pl.multiple_of(j * tile, tile), tile)
        pt = jnp.exp2(st_s[g] - m).astype(BF16)
        pv = jnp.dot(vt_ref[g * V_ROWS:(g + 1) * V_ROWS, ks], pt, preferred_element_type=F32)
        acc_s[g] = alpha * acc_s[g] + pv

    acc_s[...] = jnp.zeros_like(acc_s)
    key = lax.broadcasted_iota(jnp.int32, (tile, tile), 0)
    qry = lax.broadcasted_iota(jnp.int32, (tile, tile), 1)
    carry = []
    for g in range(group):
        st = jnp.where(key <= qry, logits(g, qi), NEG_BIG)
        st_s[g] = st
        carry.append((jnp.max(st, axis=0, keepdims=True), jnp.ones((1, tile), F32)))

    def step(i, carry):
        prev = jnp.where(i == 0, qi, i - 1)
        out = []
        for g in range(group):
            m, alpha = carry[g]
            accumulate(g, prev, m, alpha)
            st = logits(g, i)
            m_new = jnp.maximum(m, jnp.max(st, axis=0, keepdims=True))
            st_s[g] = st
            out.append((m_new, jnp.exp2(m - m_new)))
        return tuple(out)

    carry = lax.fori_loop(0, qi, step, tuple(carry))
    last = jnp.where(qi == 0, qi, qi - 1)
    for g, ln in enumerate(lanes):
        accumulate(g, last, *carry[g])
        acc = acc_s[g]
        o_ref[:, ln] = (acc[:HEAD_DIM] / acc[HEAD_DIM:HEAD_DIM + 1]).T.astype(o_ref.dtype)


def _fox(proj, qa, ka, vt, n_heads, col_q, col_k, tile, group):
    bsz, seq, _ = proj.shape
    width = group * HEAD_DIM
    assert n_heads % group == 0 and col_q % group == 0 and col_k % group == 0
    qblk = lambda c0: pl.BlockSpec((None, tile, width), lambda b, h, i: (b, i, c0 // group + h))
    kblk = lambda c0: pl.BlockSpec((None, seq, width), lambda b, h, i: (b, 0, c0 // group + h),
                                   pipeline_mode=pl.Buffered(1))
    return pl.pallas_call(
        functools.partial(_fox_body, tile=tile, group=group),
        grid=(bsz, n_heads // group, seq // tile),
        in_specs=[qblk(col_q), qblk(0), kblk(col_k), kblk(0),
                  pl.BlockSpec((None, group * V_ROWS, seq), lambda b, h, i: (b, h, 0),
                               pipeline_mode=pl.Buffered(1))],
        out_specs=pl.BlockSpec((None, tile, width), lambda b, h, i: (b, i, h)),
        out_shape=jax.ShapeDtypeStruct((bsz, seq, n_heads * HEAD_DIM), BF16),
        scratch_shapes=[pltpu.VMEM((group, V_ROWS, tile), F32), pltpu.VMEM((group, tile, tile), F32)],
        compiler_params=_params(("parallel", "parallel", "arbitrary")),
        name="fox",
    )(proj, qa, proj, ka, vt)


def _mem_kv_body(mem_ref, g_ref, w_ref, out_ref):
    mn = _rms(mem_ref[...], g_ref[...]).astype(BF16)
    out_ref[...] = jnp.dot(mn, w_ref[...], preferred_element_type=F32).astype(out_ref.dtype)


def _mem_kv(mem, g, w):
    bsz, m, d = mem.shape
    cols = w.shape[1]
    return pl.pallas_call(
        _mem_kv_body,
        grid=(bsz,),
        in_specs=[pl.BlockSpec((None, m, d), lambda b: (b, 0, 0)),
                  pl.BlockSpec((1, d), lambda b: (0, 0)),
                  pl.BlockSpec((d, cols), lambda b: (0, 0))],
        out_specs=pl.BlockSpec((None, m, cols), lambda b: (b, 0, 0)),
        out_shape=jax.ShapeDtypeStruct((bsz, m, cols), BF16),
        compiler_params=_params(("parallel",)),
        name="mem_kv",
    )(mem, g, w)


def _mem_attn_body(u_ref, wq_ref, kv_ref, o_ref, *, n_heads):
    width = n_heads * HEAD_DIM
    scale = HEAD_DIM ** -0.5
    q = (jnp.dot(u_ref[...], wq_ref[...], preferred_element_type=F32) * scale).astype(BF16)
    for h in range(n_heads):
        lo, hi = h * HEAD_DIM, (h + 1) * HEAD_DIM
        s = _dot_nt(q[:, lo:hi], kv_ref[:, lo:hi])
        p = jnp.exp(s - jnp.max(s, axis=1, keepdims=True))
        p = p / jnp.sum(p, axis=1, keepdims=True)
        o = jnp.dot(p.astype(BF16), kv_ref[:, width + lo:width + hi], preferred_element_type=F32)
        o_ref[:, lo:hi] = o.astype(o_ref.dtype)


def _mem_attn(u, w_mq, mem_kv, n_heads, tq):
    bsz, seq, d = u.shape
    width = n_heads * HEAD_DIM
    m = mem_kv.shape[1]
    return pl.pallas_call(
        functools.partial(_mem_attn_body, n_heads=n_heads),
        grid=(bsz, seq // tq),
        in_specs=[pl.BlockSpec((None, tq, d), lambda b, i: (b, i, 0)),
                  pl.BlockSpec((d, width), lambda b, i: (0, 0)),
                  pl.BlockSpec((None, m, 2 * width), lambda b, i: (b, 0, 0))],
        out_specs=pl.BlockSpec((None, tq, width), lambda b, i: (b, i, 0)),
        out_shape=jax.ShapeDtypeStruct((bsz, seq, width), BF16),
        compiler_params=_params(("parallel", "parallel")),
        name="mem_attn",
    )(u, w_mq, mem_kv)


def _merge_body(oh_ref, of_ref, om_ref, g0_ref, g1_ref, g2_ref, x_ref, post_ref,
                wh_ref, wf_ref, wm_ref, wo_ref, out_ref):
    merged = g0_ref[...].astype(F32) * jnp.dot(oh_ref[...], wh_ref[...], preferred_element_type=F32)
    merged += g1_ref[...].astype(F32) * jnp.dot(of_ref[...], wf_ref[...], preferred_element_type=F32)
    merged += g2_ref[...].astype(F32) * jnp.dot(om_ref[...], wm_ref[...], preferred_element_type=F32)
    y = jnp.dot(merged.astype(BF16), wo_ref[...], preferred_element_type=F32)
    out_ref[...] = x_ref[...] + _rms(y, post_ref[...])


def _merge(o_h, o_f, o_m, gates, x1, post, wh, wf, wm, wo, tm):
    n, d = x1.shape
    row = lambda width: pl.BlockSpec((tm, width), lambda i: (i, 0))
    gate = lambda k: pl.BlockSpec((tm, d), lambda i, k=k: (i, k))
    full = lambda w: pl.BlockSpec(w.shape, lambda i: (0, 0), pipeline_mode=pl.Buffered(1))
    return pl.pallas_call(
        _merge_body,
        grid=(n // tm,),
        in_specs=[row(o_h.shape[1]), row(o_f.shape[1]), row(o_m.shape[1]),
                  gate(0), gate(1), gate(2), row(d),
                  pl.BlockSpec((1, d), lambda i: (0, 0)),
                  full(wh), full(wf), full(wm), full(wo)],
        out_specs=row(d),
        out_shape=jax.ShapeDtypeStruct((n, d), F32),
        compiler_params=_params(("parallel",)),
        name="merge",
    )(o_h, o_f, o_m, gates, gates, gates, x1, post, wh, wf, wm, wo)


def _pad_cols(w, to):
    return jnp.pad(w, ((0, 0), (0, to - w.shape[1])))


def _ffn_weights(wg, wu, wd):
    return wg.astype(BF16), wu.astype(BF16), wd.astype(BF16)


def kernel(x, mem, ffn1_pre, ffn1_post, ffn1_wg, ffn1_wu, ffn1_wd, mix_pre, mix_post, mem_norm,
           w_in, hgrn_lb, hgrn_gnorm, fox_fb, w_mem_kv, w_hgrn_out, w_fox_out, w_mem_out, w_gate,
           w_o, ffn2_pre, ffn2_post, ffn2_wg, ffn2_wu, ffn2_wd):
    bsz, seq, d = x.shape
    n = bsz * seq
    depth = ffn1_pre.shape[0]
    wh, wf, wm = w_hgrn_out.shape[1], w_fox_out.shape[1], w_mem_out.shape[1]
    h_hgrn, h_fox, h_mem = wh // HEAD_DIM, wf // HEAD_DIM, wm // HEAD_DIM
    c_fox = 4 * wh
    c_ff = c_fox + 3 * wf
    c_mq = c_ff + h_fox
    n_gate = w_gate.shape[2]
    assert h_fox <= LANES and n_gate == 3 * d
    blk = lambda c: c // HEAD_DIM
    t = _tiles(n, seq, c_ff, n_gate, h_fox, (blk(c_fox), blk(c_fox + wf)))
    scale = np.ones((1, c_ff), np.float32)
    scale[:, c_fox:c_fox + wf] = HEAD_DIM ** -0.5 * LOG2E
    scale = jnp.asarray(scale)
    no_scale = jnp.ones((1, n_gate), F32)

    x2 = x.reshape(n, d)
    vec = lambda a: a.reshape(1, -1)
    for l in range(depth):
        wg1, wu1, wd1 = _ffn_weights(ffn1_wg[l], ffn1_wu[l], ffn1_wd[l])
        x1, u = _ffn(x2, vec(ffn1_pre[l]), vec(ffn1_post[l]), vec(mix_pre[l]), wg1, wu1, wd1,
                     t["ffn_tm"], t["ffn_tf"])
        u3 = u.reshape(bsz, seq, d)

        proj = _proj(u, w_in, l, c_ff, scale, False, t["proj_tm"], t["proj_tn"], "in_proj")
        gates = _proj(u, w_gate, l, n_gate, no_scale, True, t["proj_tm"], t["gate_tn"], "gate_proj")
        proj3 = proj.reshape(bsz, seq, -1)

        o_h = _hgrn(proj3, hgrn_lb, l, vec(hgrn_gnorm[l]), h_hgrn, 0, t["hgrn_chunk"], t["hgrn_tt"])

        w_ff = _pad_cols(w_in[l][:, c_ff:c_mq], LANES).astype(BF16)
        qa, ka, vt = _fox_prep(u3, w_ff, _pad_cols(vec(fox_fb[l]), LANES), proj3,
                               blk(c_fox + 2 * wf), h_fox, t["prep_tt"])
        o_f = _fox(proj3, qa, ka, vt, h_fox, blk(c_fox), blk(c_fox + wf), t["fox_t"], t["fox_group"])

        mkv = _mem_kv(mem, vec(mem_norm[l]), w_mem_kv[l].astype(BF16))
        o_m = _mem_attn(u3, w_in[l][:, c_mq:].astype(BF16), mkv, h_mem, t["mem_tq"])

        x2 = _merge(o_h.reshape(n, wh), o_f.reshape(n, wf), o_m.reshape(n, wm), gates,
                    x1, vec(mix_post[l]), w_hgrn_out[l].astype(BF16), w_fox_out[l].astype(BF16),
                    w_mem_out[l].astype(BF16), w_o[l].astype(BF16), t["merge_tm"])

        wg2, wu2, wd2 = _ffn_weights(ffn2_wg[l], ffn2_wu[l], ffn2_wd[l])
        x2 = _ffn(x2, vec(ffn2_pre[l]), vec(ffn2_post[l]), None, wg2, wu2, wd2,
                  t["ffn_tm"], t["ffn_tf"])
    return x2.reshape(bsz, seq, d)
```

```python
import functools
import math

import numpy as np
import jax
import jax.numpy as jnp
from jax import lax
from jax.experimental import pallas as pl
from jax.experimental.pallas import tpu as pltpu

F32 = jnp.float32
BF16 = jnp.bfloat16
EPS = 1e-6
HEAD_DIM = 128
LANES = 128
LOG2E = math.log2(math.e)
VMEM_LIMIT_BYTES = 56 * 1024 * 1024
NEG_BIG = -1e30


def _tiles(n_tokens, seq, proj_cols, gate_cols, fox_heads, fox_head_cols):
    widest = lambda cols: next(w for w in (1024, 768, 512, 256, 128) if cols % w == 0)
    fox_group = next(g for g in (3, 2, 1)
                     if fox_heads % g == 0 and all(c % g == 0 for c in fox_head_cols))
    return dict(
        ffn_tm=min(512, n_tokens), ffn_tf=1024,
        proj_tm=min(1024, n_tokens), proj_tn=widest(proj_cols), gate_tn=widest(gate_cols),
        hgrn_chunk=min(128, seq), hgrn_tt=min(1024, seq),
        prep_tt=min(256, seq),
        fox_t=min(512, seq), fox_group=fox_group,
        mem_tq=min(512, seq),
        merge_tm=min(256, n_tokens),
    )


def _params(sem):
    return pltpu.CompilerParams(dimension_semantics=sem, vmem_limit_bytes=VMEM_LIMIT_BYTES)


def _rms(x, g):
    ms = jnp.mean(x * x, axis=-1, keepdims=True)
    return x * lax.rsqrt(ms + EPS) * g


def _sigmoid(x):
    return 0.5 * jnp.tanh(0.5 * x) + 0.5


def _silu(x):
    return x * _sigmoid(x)


def _dot_nt(a, b):
    return lax.dot_general(a, b, (((1,), (1,)), ((), ())), preferred_element_type=F32)


def _dot_tn(a, b):
    return lax.dot_general(a, b, (((0,), (0,)), ((), ())), preferred_element_type=F32)


def _ffn_body(*refs, emit_next, last_width):
    if emit_next:
        x_ref, pre_ref, post_ref, nxt_ref, wg_ref, wu_ref, wd_ref, out_ref, u_ref, xn_s, acc_s = refs
    else:
        x_ref, pre_ref, post_ref, wg_ref, wu_ref, wd_ref, out_ref, xn_s, acc_s = refs
    j = pl.program_id(1)
    last = pl.num_programs(1) - 1

    @pl.when(j == 0)
    def _():
        xn_s[...] = _rms(x_ref[...], pre_ref[...]).astype(BF16)
        acc_s[...] = jnp.zeros_like(acc_s)

    def add_hidden_tile(width):
        xn = xn_s[...]
        g = jnp.dot(xn, wg_ref[:, :width], preferred_element_type=F32)
        u = jnp.dot(xn, wu_ref[:, :width], preferred_element_type=F32)
        h = (_silu(g) * u).astype(BF16)
        acc_s[...] += jnp.dot(h, wd_ref[:width, :], preferred_element_type=F32)

    @pl.when(j < last)
    def _():
        add_hidden_tile(wg_ref.shape[1])

    @pl.when(j == last)
    def _():
        add_hidden_tile(last_width)
        x1 = x_ref[...] + 0.5 * _rms(acc_s[...], post_ref[...])
        out_ref[...] = x1
        if emit_next:
            u_ref[...] = _rms(x1, nxt_ref[...]).astype(BF16)


def _ffn(x, pre, post, nxt, wg, wu, wd, tm, tf):
    n, d = x.shape
    f = wg.shape[1]
    n_f = pl.cdiv(f, tf)
    emit_next = nxt is not None
    row = lambda i, j: (i, 0)
    vec = pl.BlockSpec((1, d), lambda i, j: (0, 0))
    in_specs = [pl.BlockSpec((tm, d), row), vec, vec]
    args = [x, pre, post]
    if emit_next:
        in_specs.append(vec)
        args.append(nxt)
    in_specs += [pl.BlockSpec((d, tf), lambda i, j: (0, j)),
                 pl.BlockSpec((d, tf), lambda i, j: (0, j)),
                 pl.BlockSpec((tf, d), lambda i, j: (j, 0))]
    args += [wg, wu, wd]
    out_shape = [jax.ShapeDtypeStruct((n, d), F32)]
    out_specs = [pl.BlockSpec((tm, d), row)]
    if emit_next:
        out_shape.append(jax.ShapeDtypeStruct((n, d), BF16))
        out_specs.append(pl.BlockSpec((tm, d), row))
    res = pl.pallas_call(
        functools.partial(_ffn_body, emit_next=emit_next, last_width=f - (n_f - 1) * tf),
        grid=(n // tm, n_f),
        in_specs=in_specs, out_specs=out_specs, out_shape=out_shape,
        scratch_shapes=[pltpu.VMEM((tm, d), BF16), pltpu.VMEM((tm, d), F32)],
        compiler_params=_params(("parallel", "arbitrary")),
        name="ffn_next" if emit_next else "ffn",
    )(*args)
    return res if emit_next else res[0]


def _proj_body(u_ref, w_ref, s_ref, out_ref, wb_s, *, gate, w_transposed):
    @pl.when(pl.program_id(1) == 0)
    def _():
        w = w_ref[...]
        wb_s[...] = (w.T if w_transposed else w).astype(BF16)

    acc = jnp.dot(u_ref[...], wb_s[...], preferred_element_type=F32)
    if gate:
        out_ref[...] = _sigmoid(acc).astype(out_ref.dtype)
    else:
        out_ref[...] = (acc * s_ref[...]).astype(out_ref.dtype)


def _proj(u, w, layer, cols, scale, gate, w_transposed, tm, tn, name):
    n, d = u.shape
    assert cols % tn == 0
    if w_transposed:
        w_spec = pl.BlockSpec((None, tn, d), lambda j, i: (layer, j, 0))
    else:
        w_spec = pl.BlockSpec((None, d, tn), lambda j, i: (layer, 0, j))
    return pl.pallas_call(
        functools.partial(_proj_body, gate=gate, w_transposed=w_transposed),
        grid=(cols // tn, n // tm),
        in_specs=[pl.BlockSpec((tm, d), lambda j, i: (i, 0)), w_spec,
                  pl.BlockSpec((1, tn), lambda j, i: (0, j))],
        out_specs=pl.BlockSpec((tm, tn), lambda j, i: (i, j)),
        out_shape=jax.ShapeDtypeStruct((n, cols), BF16),
        scratch_shapes=[pltpu.VMEM((d, tn), BF16)],
        compiler_params=_params(("parallel", "arbitrary")),
        name=name,
    )(u, w, scale)


def _hgrn_levels(chunk):
    return [chunk >> (i + 1) for i in range(int(math.log2(chunk)))]


def _hgrn_constants(chunk):
    t = np.arange(chunk)[:, None]
    r = np.arange(chunk)[None, :]
    blocks = [(r <= t), (r > t)]
    masks = [(r == t)]
    for h in _hgrn_levels(chunk):
        ref = (t // (2 * h)) * (2 * h) + h - 1
        upper = (t & h) != 0
        blocks.append(np.where(upper, (r > ref) & (r <= t), (r > t) & (r <= ref)))
        masks.append(((t ^ r) >= h) & ((t ^ r) < 2 * h) & upper)
    a = np.concatenate(blocks, axis=0).astype(np.float32)
    m = np.stack(masks, axis=0).astype(np.float32)
    return jnp.asarray(np.concatenate([a, a], axis=1), BF16), jnp.asarray(m, F32)


def _hgrn_body(lb_ref, gn_ref, a_ref, mask_ref, hq_ref, hf_ref, hi_ref, hog_ref, o_ref, st_s, ex_s,
               *, chunk, n_chunks, layer):
    @pl.when(pl.program_id(2) == 0)
    def _():
        st_s[...] = jnp.zeros_like(st_s)

    slots = [lb_ref[r:r + 1, :] for r in range(lb_ref.shape[0])]
    top = functools.reduce(jnp.maximum, slots)
    es = [jnp.exp(s - top) for s in slots]
    lb = sum(es[1:layer + 1], es[0]) / sum(es[1:], es[0])
    gn = gn_ref[...]
    levels = _hgrn_levels(chunk)
    row = lax.broadcasted_iota(jnp.int32, (chunk, HEAD_DIM), 0)
    rows = [slice(c * chunk, (c + 1) * chunk) for c in range(n_chunks)]

    g = lb + (1.0 - lb) * _sigmoid(hf_ref[...].astype(F32))
    lg = jnp.log(g)
    lg_hi = lg.astype(BF16)
    lg_lo = (lg - lg_hi.astype(F32)).astype(BF16)
    pieces = jnp.concatenate([jnp.concatenate([p[r] for r in rows], axis=1) for p in (lg_hi, lg_lo)],
                             axis=0)
    ex_s[...] = jnp.exp(jnp.dot(a_ref[...], pieces, preferred_element_type=F32))

    for c, r in enumerate(rows):
        ln = slice(c * HEAD_DIM, (c + 1) * HEAD_DIM)
        v = hi_ref[r, :]
        kk = 1.0 - g[r]
        q = _silu(hq_ref[r, :].astype(F32))
        hog = hog_ref[r, :].astype(F32)
        b_exp = ex_s[0:chunk, ln]
        to_end = ex_s[chunk:2 * chunk, ln]
        st = st_s[...]
        o = _dot_nt((q * b_exp).astype(BF16), st.astype(BF16))
        scores = _dot_nt(q.astype(BF16), kk.astype(BF16)) * mask_ref[0]
        for li, h in enumerate(levels):
            f = ex_s[(2 + li) * chunk:(3 + li) * chunk, ln]
            y = (jnp.where((row & h) != 0, q, kk) * f).astype(BF16)
            scores = scores + _dot_nt(y, y) * mask_ref[1 + li]
        o = o + jnp.dot(scores.astype(BF16), v, preferred_element_type=F32)
        st_s[...] = st * ex_s[chunk - 1:chunk, ln] + _dot_tn(v, (kk * to_end).astype(BF16))
        o = _rms(o, gn) * _silu(hog)
        o_ref[r, :] = o.astype(o_ref.dtype)


def _hgrn(proj, hgrn_lb, layer, gnorm, n_heads, col0, chunk, tt):
    bsz, seq, _ = proj.shape
    a_mat, masks = _hgrn_constants(chunk)
    n_slots = hgrn_lb.shape[0]

    def col(k):
        return pl.BlockSpec((None, tt, HEAD_DIM), lambda b, h, t, k=k: (b, t, col0 + k * n_heads + h))

    return pl.pallas_call(
        functools.partial(_hgrn_body, chunk=chunk, n_chunks=tt // chunk, layer=layer),
        grid=(bsz, n_heads, seq // tt),
        in_specs=[pl.BlockSpec((n_slots, HEAD_DIM), lambda b, h, t: (0, h)),
                  pl.BlockSpec((1, HEAD_DIM), lambda b, h, t: (0, h)),
                  pl.BlockSpec(a_mat.shape, lambda b, h, t: (0, 0)),
                  pl.BlockSpec(masks.shape, lambda b, h, t: (0, 0, 0)),
                  col(0), col(1), col(2), col(3)],
        out_specs=pl.BlockSpec((None, tt, HEAD_DIM), lambda b, h, t: (b, t, h)),
        out_shape=jax.ShapeDtypeStruct((bsz, seq, n_heads * HEAD_DIM), BF16),
        scratch_shapes=[pltpu.VMEM((HEAD_DIM, HEAD_DIM), F32),
                        pltpu.VMEM((a_mat.shape[0], (tt // chunk) * HEAD_DIM), F32)],
        compiler_params=_params(("parallel", "parallel", "arbitrary")),
        name="hgrn",
    )(hgrn_lb, gnorm, a_mat, masks, proj, proj, proj, proj)


def _split3(x):
    hi = x.astype(BF16).astype(F32)
    r = x - hi
    mid = r.astype(BF16).astype(F32)
    lo = (r - mid).astype(BF16).astype(F32)
    return hi, mid, lo


V_ROWS = HEAD_DIM + 16


def _fox_bias_placement(n_heads):
    width = n_heads * LANES
    place = np.zeros((3 * LANES, 2 * width), np.float32)
    const = np.zeros((1, 2 * width), np.float32)
    for h in range(n_heads):
        for p in range(3):
            place[p * LANES + h, h * LANES + p] = 1.0
            place[p * LANES + h, width + h * LANES + 3 + p] = -1.0
        const[0, h * LANES + 3:h * LANES + 6] = 1.0
        const[0, width + h * LANES:width + h * LANES + 3] = 1.0
    return jnp.asarray(place, BF16), jnp.asarray(const, F32)


def _fox_prep_body(u_ref, wff_ref, fb_ref, tri_ref, place_ref, const_ref, v_ref, qa_ref, ka_ref, vt_ref,
                   carry_s, *, n_heads):
    @pl.when(pl.program_id(1) == 0)
    def _():
        carry_s[...] = jnp.zeros_like(carry_s)

    tt = u_ref.shape[0]
    width = n_heads * LANES
    z = jnp.dot(u_ref[...], wff_ref[...], preferred_element_type=F32) + fb_ref[...]
    log_f = jnp.minimum(z, 0.0) - jnp.log(1.0 + jnp.exp(-jnp.abs(z)))
    tri = tri_ref[...]
    c = carry_s[...]
    for piece in _split3(log_f):
        c = c + jnp.dot(tri, piece.astype(BF16), preferred_element_type=F32)
    carry_s[...] = c[tt - 1:tt, :]
    pieces = jnp.concatenate(_split3(c * LOG2E), axis=1).astype(BF16)
    bias = jnp.dot(pieces, place_ref[...], preferred_element_type=F32) + const_ref[...]
    qa_ref[...] = bias[:, :width].astype(BF16)
    ka_ref[...] = bias[:, width:].astype(BF16)
    ones = jnp.ones((V_ROWS - HEAD_DIM, tt), BF16)
    for h in range(n_heads):
        vt_ref[h * V_ROWS:h * V_ROWS + HEAD_DIM, :] = (
            v_ref[:, h * LANES:(h + 1) * LANES].astype(F32).T.astype(BF16))
        vt_ref[h * V_ROWS + HEAD_DIM:(h + 1) * V_ROWS, :] = ones


def _fox_prep(u, w_ff, fox_b, proj, col_v, n_heads, tt):
    bsz, seq, d = u.shape
    width = n_heads * LANES
    assert col_v % n_heads == 0
    tri = jnp.asarray(np.tril(np.ones((tt, tt), np.float32)), BF16)
    place, const = _fox_bias_placement(n_heads)
    out = jax.ShapeDtypeStruct((bsz, seq, width), BF16)
    blk = pl.BlockSpec((None, tt, width), lambda b, t: (b, t, 0))
    whole = lambda a: pl.BlockSpec(a.shape, lambda b, t: (0, 0))
    return pl.pallas_call(
        functools.partial(_fox_prep_body, n_heads=n_heads),
        grid=(bsz, seq // tt),
        in_specs=[pl.BlockSpec((None, tt, d), lambda b, t: (b, t, 0)),
                  whole(w_ff), whole(fox_b), whole(tri), whole(place), whole(const),
                  pl.BlockSpec((None, tt, width), lambda b, t: (b, t, col_v // n_heads))],
        out_specs=[blk, blk, pl.BlockSpec((None, n_heads * V_ROWS, tt), lambda b, t: (b, 0, t))],
        out_shape=[out, out, jax.ShapeDtypeStruct((bsz, n_heads * V_ROWS, seq), BF16)],
        scratch_shapes=[pltpu.VMEM((1, LANES), F32)],
        compiler_params=_params(("parallel", "arbitrary")),
        name="fox_prep",
    )(u, w_ff, fox_b, tri, place, const, proj)


def _fox_body(q_ref, qa_ref, k_ref, ka_ref, vt_ref, o_ref, acc_s, st_s, *, tile, group):
    qi = pl.program_id(2)
    lanes = [slice(g * HEAD_DIM, (g + 1) * HEAD_DIM) for g in range(group)]
    qcs = [jnp.concatenate([q_ref[:, ln], qa_ref[:, ln]], axis=1) for ln in lanes]

    def logits(g, j):
        ks = pl.ds(pl.multiple_of(j * tile, tile), tile)
        kc = jnp.concatenate([k_ref[ks, lanes[g]], ka_ref[ks, lanes[g]]], axis=1)
        return _dot_nt(kc, qcs[g])

    def accumulate(g, j, m, alpha):
        ks = pl.ds(pl.multiple_of(j * tile, tile), tile)
        pt = jnp.exp2(st_s[g] - m).astype(BF16)
        pv = jnp.dot(vt_ref[g * V_ROWS:(g + 1) * V_ROWS, ks], pt, preferred_element_type=F32)
        acc_s[g] = alpha * acc_s[g] + pv

    acc_s[...] = jnp.zeros_like(acc_s)
    key = lax.broadcasted_iota(jnp.int32, (tile, tile), 0)
    qry = lax.broadcasted_iota(jnp.int32, (tile, tile), 1)
    carry = []
    for g in range(group):
        st = jnp.where(key <= qry, logits(g, qi), NEG_BIG)
        st_s[g] = st
        carry.append((jnp.max(st, axis=0, keepdims=True), jnp.ones((1, tile), F32)))

    def step(i, carry):
        prev = jnp.where(i == 0, qi, i - 1)
        out = []
        for g in range(group):
            m, alpha = carry[g]
            accumulate(g, prev, m, alpha)
            st = logits(g, i)
            m_new = jnp.maximum(m, jnp.max(st, axis=0, keepdims=True))
            st_s[g] = st
            out.append((m_new, jnp.exp2(m - m_new)))
        return tuple(out)

    carry = lax.fori_loop(0, qi, step, tuple(carry))
    last = jnp.where(qi == 0, qi, qi - 1)
    for g, ln in enumerate(lanes):
        accumulate(g, last, *carry[g])
        acc = acc_s[g]
        o_ref[:, ln] = (acc[:HEAD_DIM] / acc[HEAD_DIM:HEAD_DIM + 1]).T.astype(o_ref.dtype)


def _fox(proj, qa, ka, vt, n_heads, col_q, col_k, tile, group):
    bsz, seq, _ = proj.shape
    width = group * HEAD_DIM
    assert n_heads % group == 0 and col_q % group == 0 and col_k % group == 0
    qblk = lambda c0: pl.BlockSpec((None, tile, width), lambda b, h, i: (b, i, c0 // group + h))
    kblk = lambda c0: pl.BlockSpec((None, seq, width), lambda b, h, i: (b, 0, c0 // group + h),
                                   pipeline_mode=pl.Buffered(1))
    return pl.pallas_call(
        functools.partial(_fox_body, tile=tile, group=group),
        grid=(bsz, n_heads // group, seq // tile),
        in_specs=[qblk(col_q), qblk(0), kblk(col_k), kblk(0),
                  pl.BlockSpec((None, group * V_ROWS, seq), lambda b, h, i: (b, h, 0),
                               pipeline_mode=pl.Buffered(1))],
        out_specs=pl.BlockSpec((None, tile, width), lambda b, h, i: (b, i, h)),
        out_shape=jax.ShapeDtypeStruct((bsz, seq, n_heads * HEAD_DIM), BF16),
        scratch_shapes=[pltpu.VMEM((group, V_ROWS, tile), F32), pltpu.VMEM((group, tile, tile), F32)],
        compiler_params=_params(("parallel", "parallel", "arbitrary")),
        name="fox",
    )(proj, qa, proj, ka, vt)


def _mem_kv_body(mem_ref, g_ref, w_ref, out_ref):
    mn = _rms(mem_ref[...], g_ref[...]).astype(BF16)
    out_ref[...] = jnp.dot(mn, w_ref[...], preferred_element_type=F32).astype(out_ref.dtype)


def _mem_kv(mem, g, w):
    bsz, m, d = mem.shape
    cols = w.shape[1]
    return pl.pallas_call(
        _mem_kv_body,
        grid=(bsz,),
        in_specs=[pl.BlockSpec((None, m, d), lambda b: (b, 0, 0)),
                  pl.BlockSpec((1, d), lambda b: (0, 0)),
                  pl.BlockSpec((d, cols), lambda b: (0, 0))],
        out_specs=pl.BlockSpec((None, m, cols), lambda b: (b, 0, 0)),
        out_shape=jax.ShapeDtypeStruct((bsz, m, cols), BF16),
        compiler_params=_params(("parallel",)),
        name="mem_kv",
    )(mem, g, w)


def _mem_attn_body(u_ref, wq_ref, kv_ref, o_ref, *, n_heads):
    width = n_heads * HEAD_DIM
    scale = HEAD_DIM ** -0.5
    q = (jnp.dot(u_ref[...], wq_ref[...], preferred_element_type=F32) * scale).astype(BF16)
    for h in range(n_heads):
        lo, hi = h * HEAD_DIM, (h + 1) * HEAD_DIM
        s = _dot_nt(q[:, lo:hi], kv_ref[:, lo:hi])
        p = jnp.exp(s - jnp.max(s, axis=1, keepdims=True))
        p = p / jnp.sum(p, axis=1, keepdims=True)
        o = jnp.dot(p.astype(BF16), kv_ref[:, width + lo:width + hi], preferred_element_type=F32)
        o_ref[:, lo:hi] = o.astype(o_ref.dtype)


def _mem_attn(u, w_mq, mem_kv, n_heads, tq):
    bsz, seq, d = u.shape
    width = n_heads * HEAD_DIM
    m = mem_kv.shape[1]
    return pl.pallas_call(
        functools.partial(_mem_attn_body, n_heads=n_heads),
        grid=(bsz, seq // tq),
        in_specs=[pl.BlockSpec((None, tq, d), lambda b, i: (b, i, 0)),
                  pl.BlockSpec((d, width), lambda b, i: (0, 0)),
                  pl.BlockSpec((None, m, 2 * width), lambda b, i: (b, 0, 0))],
        out_specs=pl.BlockSpec((None, tq, width), lambda b, i: (b, i, 0)),
        out_shape=jax.ShapeDtypeStruct((bsz, seq, width), BF16),
        compiler_params=_params(("parallel", "parallel")),
        name="mem_attn",
    )(u, w_mq, mem_kv)


def _merge_body(oh_ref, of_ref, om_ref, g0_ref, g1_ref, g2_ref, x_ref, post_ref,
                wh_ref, wf_ref, wm_ref, wo_ref, out_ref):
    merged = g0_ref[...].astype(F32) * jnp.dot(oh_ref[...], wh_ref[...], preferred_element_type=F32)
    merged += g1_ref[...].astype(F32) * jnp.dot(of_ref[...], wf_ref[...], preferred_element_type=F32)
    merged += g2_ref[...].astype(F32) * jnp.dot(om_ref[...], wm_ref[...], preferred_element_type=F32)
    y = jnp.dot(merged.astype(BF16), wo_ref[...], preferred_element_type=F32)
    out_ref[...] = x_ref[...] + _rms(y, post_ref[...])


def _merge(o_h, o_f, o_m, gates, x1, post, wh, wf, wm, wo, tm):
    n, d = x1.shape
    row = lambda width: pl.BlockSpec((tm, width), lambda i: (i, 0))
    gate = lambda k: pl.BlockSpec((tm, d), lambda i, k=k: (i, k))
    full = lambda w: pl.BlockSpec(w.shape, lambda i: (0, 0), pipeline_mode=pl.Buffered(1))
    return pl.pallas_call(
        _merge_body,
        grid=(n // tm,),
        in_specs=[row(o_h.shape[1]), row(o_f.shape[1]), row(o_m.shape[1]),
                  gate(0), gate(1), gate(2), row(d),
                  pl.BlockSpec((1, d), lambda i: (0, 0)),
                  full(wh), full(wf), full(wm), full(wo)],
        out_specs=row(d),
        out_shape=jax.ShapeDtypeStruct((n, d), F32),
        compiler_params=_params(("parallel",)),
        name="merge",
    )(o_h, o_f, o_m, gates, gates, gates, x1, post, wh, wf, wm, wo)


def _pad_cols(w, to):
    return jnp.pad(w, ((0, 0), (0, to - w.shape[1])))


def _ffn_weights(wg, wu, wd):
    return wg.astype(BF16), wu.astype(BF16), wd.astype(BF16)


def kernel(x, mem, ffn1_pre, ffn1_post, ffn1_wg, ffn1_wu, ffn1_wd, mix_pre, mix_post, mem_norm,
           w_in, hgrn_lb, hgrn_gnorm, fox_fb, w_mem_kv, w_hgrn_out, w_fox_out, w_mem_out, w_gate,
           w_o, ffn2_pre, ffn2_post, ffn2_wg, ffn2_wu, ffn2_wd):
    bsz, seq, d = x.shape
    n = bsz * seq
    depth = ffn1_pre.shape[0]
    wh, wf, wm = w_hgrn_out.shape[1], w_fox_out.shape[1], w_mem_out.shape[1]
    h_hgrn, h_fox, h_mem = wh // HEAD_DIM, wf // HEAD_DIM, wm // HEAD_DIM
    c_fox = 4 * wh
    c_ff = c_fox + 3 * wf
    c_mq = c_ff + h_fox
    n_gate = w_gate.shape[2]
    assert h_fox <= LANES and n_gate == 3 * d
    blk = lambda c: c // HEAD_DIM
    t = _tiles(n, seq, c_ff, n_gate, h_fox, (blk(c_fox), blk(c_fox + wf)))
    scale = np.ones((1, c_ff), np.float32)
    scale[:, c_fox:c_fox + wf] = HEAD_DIM ** -0.5 * LOG2E
    scale = jnp.asarray(scale)
    no_scale = jnp.ones((1, n_gate), F32)

    x2 = x.reshape(n, d)
    vec = lambda a: a.reshape(1, -1)
    w_in_t = jnp.swapaxes(w_in, 1, 2)
    for l in range(depth):
        wg1, wu1, wd1 = _ffn_weights(ffn1_wg[l], ffn1_wu[l], ffn1_wd[l])
        x1, u = _ffn(x2, vec(ffn1_pre[l]), vec(ffn1_post[l]), vec(mix_pre[l]), wg1, wu1, wd1,
                     t["ffn_tm"], t["ffn_tf"])
        u3 = u.reshape(bsz, seq, d)

        proj = _proj(u, w_in_t, l, c_ff, scale, False, True, t["proj_tm"], t["proj_tn"], "in_proj")
        gates = _proj(u, w_gate, l, n_gate, no_scale, True, False, t["proj_tm"], t["gate_tn"],
                      "gate_proj")
        proj3 = proj.reshape(bsz, seq, -1)

        o_h = _hgrn(proj3, hgrn_lb, l, vec(hgrn_gnorm[l]), h_hgrn, 0, t["hgrn_chunk"], t["hgrn_tt"])

        w_tail = w_in_t[l, c_ff:, :].T
        w_ff = _pad_cols(w_tail[:, :h_fox], LANES).astype(BF16)
        qa, ka, vt = _fox_prep(u3, w_ff, _pad_cols(vec(fox_fb[l]), LANES), proj3,
                               blk(c_fox + 2 * wf), h_fox, t["prep_tt"])
        o_f = _fox(proj3, qa, ka, vt, h_fox, blk(c_fox), blk(c_fox + wf), t["fox_t"], t["fox_group"])

        mkv = _mem_kv(mem, vec(mem_norm[l]), w_mem_kv[l].astype(BF16))
        o_m = _mem_attn(u3, w_tail[:, h_fox:].astype(BF16), mkv, h_mem, t["mem_tq"])

        x2 = _merge(o_h.reshape(n, wh), o_f.reshape(n, wf), o_m.reshape(n, wm), gates,
                    x1, vec(mix_post[l]), w_hgrn_out[l].astype(BF16), w_fox_out[l].astype(BF16),
                    w_mem_out[l].astype(BF16), w_o[l].astype(BF16), t["merge_tm"])

        wg2, wu2, wd2 = _ffn_weights(ffn2_wg[l], ffn2_wu[l], ffn2_wd[l])
        x2 = _ffn(x2, vec(ffn2_pre[l]), vec(ffn2_post[l]), None, wg2, wu2, wd2,
                  t["ffn_tm"], t["ffn_tf"])
    return x2.reshape(bsz, seq, d)
```

```python
import functools
import math

import numpy as np
import jax
import jax.numpy as jnp
from jax import lax
from jax.experimental import pallas as pl
from jax.experimental.pallas import tpu as pltpu

F32 = jnp.float32
BF16 = jnp.bfloat16
EPS = 1e-6
HEAD_DIM = 128
LANES = 128
LOG2E = math.log2(math.e)
VMEM_LIMIT_BYTES = 56 * 1024 * 1024
NEG_BIG = -1e30


def _tiles(n_tokens, seq, proj_cols, gate_cols, fox_heads, fox_head_cols):
    widest = lambda cols: next(w for w in (1024, 768, 512, 256, 128) if cols % w == 0)
    fox_group = next(g for g in (6, 3, 2, 1)
                     if fox_heads % g == 0 and all(c % g == 0 for c in fox_head_cols))
    return dict(
        ffn_tm=min(512, n_tokens), ffn_tf=1024,
        proj_tm=min(1024, n_tokens), proj_tn=widest(proj_cols), gate_tn=widest(gate_cols),
        hgrn_chunk=min(128, seq), hgrn_tt=min(1024, seq),
        prep_tt=min(256, seq),
        fox_t=min(512, seq), fox_group=fox_group,
        mem_tq=min(512, seq),
        merge_tm=min(256, n_tokens),
    )


def _params(sem):
    return pltpu.CompilerParams(dimension_semantics=sem, vmem_limit_bytes=VMEM_LIMIT_BYTES)


def _rms(x, g):
    ms = jnp.mean(x * x, axis=-1, keepdims=True)
    return x * lax.rsqrt(ms + EPS) * g


def _sigmoid(x):
    return 0.5 * jnp.tanh(0.5 * x) + 0.5


def _silu(x):
    return x * _sigmoid(x)


def _dot_nt(a, b):
    return lax.dot_general(a, b, (((1,), (1,)), ((), ())), preferred_element_type=F32)


def _dot_tn(a, b):
    return lax.dot_general(a, b, (((0,), (0,)), ((), ())), preferred_element_type=F32)


def _ffn_body(*refs, emit_next, ragged_width, n_tiles):
    if emit_next:
        x_ref, pre_ref, post_ref, nxt_ref, wg_ref, wu_ref, wd_ref, out_ref, u_ref, xn_s, acc_s = refs
    else:
        x_ref, pre_ref, post_ref, wg_ref, wu_ref, wd_ref, out_ref, xn_s, acc_s = refs
    j = pl.program_id(1)
    last = pl.num_programs(1) - 1
    full_width = wg_ref.shape[1]

    def hidden_tile(width):
        xn = xn_s[...]
        g = jnp.dot(xn, wg_ref[:, :width], preferred_element_type=F32)
        u = jnp.dot(xn, wu_ref[:, :width], preferred_element_type=F32)
        h = (_silu(g) * u).astype(BF16)
        return jnp.dot(h, wd_ref[:width, :], preferred_element_type=F32)

    @pl.when(j == 0)
    def _():
        xn_s[...] = _rms(x_ref[...], pre_ref[...]).astype(BF16)
        acc_s[...] = hidden_tile(ragged_width)

    @pl.when(jnp.logical_and(j > 0, j < last))
    def _():
        acc_s[...] += hidden_tile(full_width)

    @pl.when(j == last)
    def _():
        if n_tiles > 1:
            acc_s[...] += hidden_tile(full_width)
        x1 = x_ref[...] + 0.5 * _rms(acc_s[...], post_ref[...])
        out_ref[...] = x1
        if emit_next:
            u_ref[...] = _rms(x1, nxt_ref[...]).astype(BF16)


def _ffn(x, pre, post, nxt, wg, wu, wd, tm, tf):
    n, d = x.shape
    f = wg.shape[1]
    n_f = pl.cdiv(f, tf)
    emit_next = nxt is not None
    row = lambda i, j: (i, 0)
    vec = pl.BlockSpec((1, d), lambda i, j: (0, 0))
    in_specs = [pl.BlockSpec((tm, d), row), vec, vec]
    args = [x, pre, post]
    if emit_next:
        in_specs.append(vec)
        args.append(nxt)
    tile = lambda j: (j + n_f - 1) % n_f
    in_specs += [pl.BlockSpec((d, tf), lambda i, j: (0, tile(j))),
                 pl.BlockSpec((d, tf), lambda i, j: (0, tile(j))),
                 pl.BlockSpec((tf, d), lambda i, j: (tile(j), 0))]
    args += [wg, wu, wd]
    out_shape = [jax.ShapeDtypeStruct((n, d), F32)]
    out_specs = [pl.BlockSpec((tm, d), row)]
    if emit_next:
        out_shape.append(jax.ShapeDtypeStruct((n, d), BF16))
        out_specs.append(pl.BlockSpec((tm, d), row))
    res = pl.pallas_call(
        functools.partial(_ffn_body, emit_next=emit_next, ragged_width=f - (n_f - 1) * tf,
                          n_tiles=n_f),
        grid=(n // tm, n_f),
        in_specs=in_specs, out_specs=out_specs, out_shape=out_shape,
        scratch_shapes=[pltpu.VMEM((tm, d), BF16), pltpu.VMEM((tm, d), F32)],
        compiler_params=_params(("parallel", "arbitrary")),
        name="ffn_next" if emit_next else "ffn",
    )(*args)
    return res if emit_next else res[0]


def _proj_body(u_ref, w_ref, s_ref, out_ref, wb_s, *, gate):
    @pl.when(pl.program_id(1) == 0)
    def _():
        wb_s[...] = w_ref[...].astype(BF16)

    acc = jnp.dot(u_ref[...], wb_s[...], preferred_element_type=F32)
    if gate:
        out_ref[...] = _sigmoid(acc).astype(out_ref.dtype)
    else:
        out_ref[...] = (acc * s_ref[...]).astype(out_ref.dtype)


def _proj(u, w, layer, cols, scale, gate, tm, tn, name):
    n, d = u.shape
    assert cols % tn == 0
    return pl.pallas_call(
        functools.partial(_proj_body, gate=gate),
        grid=(cols // tn, n // tm),
        in_specs=[pl.BlockSpec((tm, d), lambda j, i: (i, 0)),
                  pl.BlockSpec((None, d, tn), lambda j, i: (layer, 0, j)),
                  pl.BlockSpec((1, tn), lambda j, i: (0, j))],
        out_specs=pl.BlockSpec((tm, tn), lambda j, i: (i, j)),
        out_shape=jax.ShapeDtypeStruct((n, cols), BF16),
        scratch_shapes=[pltpu.VMEM((d, tn), BF16)],
        compiler_params=_params(("parallel", "arbitrary")),
        name=name,
    )(u, w, scale)


def _hgrn_levels(chunk):
    return [chunk >> (i + 1) for i in range(int(math.log2(chunk)))]


def _hgrn_constants(chunk):
    t = np.arange(chunk)[:, None]
    r = np.arange(chunk)[None, :]
    blocks = [(r <= t), (r > t)]
    masks = [(r == t)]
    for h in _hgrn_levels(chunk):
        ref = (t // (2 * h)) * (2 * h) + h - 1
        upper = (t & h) != 0
        blocks.append(np.where(upper, (r > ref) & (r <= t), (r > t) & (r <= ref)))
        masks.append(((t ^ r) >= h) & ((t ^ r) < 2 * h) & upper)
    a = np.concatenate(blocks, axis=0).astype(np.float32)
    m = np.stack(masks, axis=0).astype(np.float32)
    return jnp.asarray(np.concatenate([a, a], axis=1), BF16), jnp.asarray(m, F32)


def _hgrn_body(lb_ref, gn_ref, a_ref, mask_ref, hq_ref, hf_ref, hi_ref, hog_ref, o_ref, st_s, ex_s,
               *, chunk, n_chunks, layer):
    @pl.when(pl.program_id(2) == 0)
    def _():
        st_s[...] = jnp.zeros_like(st_s)

    slots = [lb_ref[r:r + 1, :] for r in range(lb_ref.shape[0])]
    top = functools.reduce(jnp.maximum, slots)
    es = [jnp.exp(s - top) for s in slots]
    lb = sum(es[1:layer + 1], es[0]) / sum(es[1:], es[0])
    gn = gn_ref[...]
    levels = _hgrn_levels(chunk)
    row = lax.broadcasted_iota(jnp.int32, (chunk, HEAD_DIM), 0)
    rows = [slice(c * chunk, (c + 1) * chunk) for c in range(n_chunks)]

    g = lb + (1.0 - lb) * _sigmoid(hf_ref[...].astype(F32))
    lg = jnp.log(g)
    lg_hi = lg.astype(BF16)
    lg_lo = (lg - lg_hi.astype(F32)).astype(BF16)
    pieces = jnp.concatenate([jnp.concatenate([p[r] for r in rows], axis=1) for p in (lg_hi, lg_lo)],
                             axis=0)
    ex_s[...] = jnp.exp(jnp.dot(a_ref[...], pieces, preferred_element_type=F32))

    for c, r in enumerate(rows):
        ln = slice(c * HEAD_DIM, (c + 1) * HEAD_DIM)
        v = hi_ref[r, :]
        kk = 1.0 - g[r]
        q = _silu(hq_ref[r, :].astype(F32))
        hog = hog_ref[r, :].astype(F32)
        b_exp = ex_s[0:chunk, ln]
        to_end = ex_s[chunk:2 * chunk, ln]
        st = st_s[...]
        o = _dot_nt((q * b_exp).astype(BF16), st.astype(BF16))
        scores = _dot_nt(q.astype(BF16), kk.astype(BF16)) * mask_ref[0]
        for li, h in enumerate(levels):
            f = ex_s[(2 + li) * chunk:(3 + li) * chunk, ln]
            y = (jnp.where((row & h) != 0, q, kk) * f).astype(BF16)
            scores = scores + _dot_nt(y, y) * mask_ref[1 + li]
        o = o + jnp.dot(scores.astype(BF16), v, preferred_element_type=F32)
        st_s[...] = st * ex_s[chunk - 1:chunk, ln] + _dot_tn(v, (kk * to_end).astype(BF16))
        o = _rms(o, gn) * _silu(hog)
        o_ref[r, :] = o.astype(o_ref.dtype)


def _hgrn(proj, hgrn_lb, layer, gnorm, n_heads, col0, chunk, tt):
    bsz, seq, _ = proj.shape
    a_mat, masks = _hgrn_constants(chunk)
    n_slots = hgrn_lb.shape[0]

    def col(k):
        return pl.BlockSpec((None, tt, HEAD_DIM), lambda b, h, t, k=k: (b, t, col0 + k * n_heads + h))

    return pl.pallas_call(
        functools.partial(_hgrn_body, chunk=chunk, n_chunks=tt // chunk, layer=layer),
        grid=(bsz, n_heads, seq // tt),
        in_specs=[pl.BlockSpec((n_slots, HEAD_DIM), lambda b, h, t: (0, h)),
                  pl.BlockSpec((1, HEAD_DIM), lambda b, h, t: (0, h)),
                  pl.BlockSpec(a_mat.shape, lambda b, h, t: (0, 0)),
                  pl.BlockSpec(masks.shape, lambda b, h, t: (0, 0, 0)),
                  col(0), col(1), col(2), col(3)],
        out_specs=pl.BlockSpec((None, tt, HEAD_DIM), lambda b, h, t: (b, t, h)),
        out_shape=jax.ShapeDtypeStruct((bsz, seq, n_heads * HEAD_DIM), BF16),
        scratch_shapes=[pltpu.VMEM((HEAD_DIM, HEAD_DIM), F32),
                        pltpu.VMEM((a_mat.shape[0], (tt // chunk) * HEAD_DIM), F32)],
        compiler_params=_params(("parallel", "parallel", "arbitrary")),
        name="hgrn",
    )(hgrn_lb, gnorm, a_mat, masks, proj, proj, proj, proj)


def _split3(x):
    hi = x.astype(BF16).astype(F32)
    r = x - hi
    mid = r.astype(BF16).astype(F32)
    lo = (r - mid).astype(BF16).astype(F32)
    return hi, mid, lo


V_ROWS = HEAD_DIM + 16


def _fox_bias_placement(n_heads):
    width = n_heads * LANES
    assert 6 * n_heads <= LANES
    place = np.zeros((3 * LANES, width + LANES), np.float32)
    const = np.zeros((1, width + LANES), np.float32)
    for h in range(n_heads):
        for p in range(3):
            place[p * LANES + h, h * LANES + 6 * h + p] = 1.0
            place[p * LANES + h, width + 6 * h + 3 + p] = -1.0
        const[0, h * LANES + 6 * h + 3:h * LANES + 6 * h + 6] = 1.0
        const[0, width + 6 * h:width + 6 * h + 3] = 1.0
    return jnp.asarray(place, BF16), jnp.asarray(const, F32)


def _fox_prep_body(u_ref, wff_ref, fb_ref, tri_ref, place_ref, const_ref, v_ref, qa_ref, ka_ref, vt_ref,
                   carry_s, *, n_heads):
    @pl.when(pl.program_id(1) == 0)
    def _():
        carry_s[...] = jnp.zeros_like(carry_s)

    tt = u_ref.shape[0]
    width = n_heads * LANES
    z = jnp.dot(u_ref[...], wff_ref[...], preferred_element_type=F32) + fb_ref[...]
    log_f = jnp.minimum(z, 0.0) - jnp.log(1.0 + jnp.exp(-jnp.abs(z)))
    tri = tri_ref[...]
    c = carry_s[...]
    for piece in _split3(log_f):
        c = c + jnp.dot(tri, piece.astype(BF16), preferred_element_type=F32)
    carry_s[...] = c[tt - 1:tt, :]
    pieces = jnp.concatenate(_split3(c * LOG2E), axis=1).astype(BF16)
    bias = jnp.dot(pieces, place_ref[...], preferred_element_type=F32) + const_ref[...]
    qa_ref[...] = bias[:, :width].astype(BF16)
    ka_ref[...] = bias[:, width:].astype(BF16)
    ones = jnp.ones((V_ROWS - HEAD_DIM, tt), BF16)
    for h in range(n_heads):
        vt_ref[h * V_ROWS:h * V_ROWS + HEAD_DIM, :] = (
            v_ref[:, h * LANES:(h + 1) * LANES].astype(F32).T.astype(BF16))
        vt_ref[h * V_ROWS + HEAD_DIM:(h + 1) * V_ROWS, :] = ones


def _fox_prep(u, w_ff, fox_b, proj, col_v, n_heads, tt):
    bsz, seq, d = u.shape
    width = n_heads * LANES
    assert col_v % n_heads == 0
    tri = jnp.asarray(np.tril(np.ones((tt, tt), np.float32)), BF16)
    place, const = _fox_bias_placement(n_heads)
    out = jax.ShapeDtypeStruct((bsz, seq, width), BF16)
    blk = pl.BlockSpec((None, tt, width), lambda b, t: (b, t, 0))
    whole = lambda a: pl.BlockSpec(a.shape, lambda b, t: (0, 0))
    return pl.pallas_call(
        functools.partial(_fox_prep_body, n_heads=n_heads),
        grid=(bsz, seq // tt),
        in_specs=[pl.BlockSpec((None, tt, d), lambda b, t: (b, t, 0)),
                  whole(w_ff), whole(fox_b), whole(tri), whole(place), whole(const),
                  pl.BlockSpec((None, tt, width), lambda b, t: (b, t, col_v // n_heads))],
        out_specs=[blk, pl.BlockSpec((None, tt, LANES), lambda b, t: (b, t, 0)),
                   pl.BlockSpec((None, n_heads * V_ROWS, tt), lambda b, t: (b, 0, t))],
        out_shape=[out, jax.ShapeDtypeStruct((bsz, seq, LANES), BF16),
                   jax.ShapeDtypeStruct((bsz, n_heads * V_ROWS, seq), BF16)],
        scratch_shapes=[pltpu.VMEM((1, LANES), F32)],
        compiler_params=_params(("parallel", "arbitrary")),
        name="fox_prep",
    )(u, w_ff, fox_b, tri, place, const, proj)


def _fox_body(q_ref, qa_ref, k_ref, ka_ref, vt_ref, o_ref, acc_s, st_s, *, tile, group):
    qi = pl.program_id(2)
    lanes = [slice(g * HEAD_DIM, (g + 1) * HEAD_DIM) for g in range(group)]
    qcs = [jnp.concatenate([q_ref[:, ln], qa_ref[:, ln]], axis=1) for ln in lanes]

    def logits(g, j):
        ks = pl.ds(pl.multiple_of(j * tile, tile), tile)
        kc = jnp.concatenate([k_ref[ks, lanes[g]], ka_ref[ks, :]], axis=1)
        return _dot_nt(kc, qcs[g])

    def accumulate(g, j, m, alpha):
        ks = pl.ds(pl.multiple_of(j * tile, tile), tile)
        pt = jnp.exp2(st_s[g] - m).astype(BF16)
        pv = jnp.dot(vt_ref[g * V_ROWS:(g + 1) * V_ROWS, ks], pt, preferred_element_type=F32)
        acc_s[g] = alpha * acc_s[g] + pv

    acc_s[...] = jnp.zeros_like(acc_s)
    key = lax.broadcasted_iota(jnp.int32, (tile, tile), 0)
    qry = lax.broadcasted_iota(jnp.int32, (tile, tile), 1)
    carry = []
    for g in range(group):
        st = jnp.where(key <= qry, logits(g, qi), NEG_BIG)
        st_s[g] = st
        carry.append((jnp.max(st, axis=0, keepdims=True), jnp.ones((1, tile), F32)))

    def step(i, carry):
        prev = jnp.where(i == 0, qi, i - 1)
        out = []
        for g in range(group):
            m, alpha = carry[g]
            accumulate(g, prev, m, alpha)
            st = logits(g, i)
            m_new = jnp.maximum(m, jnp.max(st, axis=0, keepdims=True))
            st_s[g] = st
            out.append((m_new, jnp.exp2(m - m_new)))
        return tuple(out)

    carry = lax.fori_loop(0, qi, step, tuple(carry))
    last = jnp.where(qi == 0, qi, qi - 1)
    for g, ln in enumerate(lanes):
        accumulate(g, last, *carry[g])
        acc = acc_s[g]
        o_ref[:, ln] = (acc[:HEAD_DIM] / acc[HEAD_DIM:HEAD_DIM + 1]).T.astype(o_ref.dtype)


def _fox(proj, qa, ka, vt, n_heads, col_q, col_k, tile, group):
    bsz, seq, _ = proj.shape
    width = group * HEAD_DIM
    assert n_heads % group == 0 and col_q % group == 0 and col_k % group == 0
    qblk = lambda c0: pl.BlockSpec((None, tile, width), lambda b, h, i: (b, i, c0 // group + h))
    kblk = lambda c0: pl.BlockSpec((None, seq, width), lambda b, h, i: (b, 0, c0 // group + h),
                                   pipeline_mode=pl.Buffered(1))
    return pl.pallas_call(
        functools.partial(_fox_body, tile=tile, group=group),
        grid=(bsz, n_heads // group, seq // tile),
        in_specs=[qblk(col_q), qblk(0), kblk(col_k),
                  pl.BlockSpec((None, seq, LANES), lambda b, h, i: (b, 0, 0), pipeline_mode=pl.Buffered(1)),
                  pl.BlockSpec((None, group * V_ROWS, seq), lambda b, h, i: (b, h, 0),
                               pipeline_mode=pl.Buffered(1))],
        out_specs=pl.BlockSpec((None, tile, width), lambda b, h, i: (b, i, h)),
        out_shape=jax.ShapeDtypeStruct((bsz, seq, n_heads * HEAD_DIM), BF16),
        scratch_shapes=[pltpu.VMEM((group, V_ROWS, tile), F32), pltpu.VMEM((group, tile, tile), F32)],
        compiler_params=_params(("parallel", "parallel", "arbitrary")),
        name="fox",
    )(proj, qa, proj, ka, vt)


def _mem_kv_body(mem_ref, g_ref, w_ref, out_ref):
    mn = _rms(mem_ref[...], g_ref[...]).astype(BF16)
    out_ref[...] = jnp.dot(mn, w_ref[...], preferred_element_type=F32).astype(out_ref.dtype)


def _mem_kv(mem, g, w):
    bsz, m, d = mem.shape
    cols = w.shape[1]
    return pl.pallas_call(
        _mem_kv_body,
        grid=(bsz,),
        in_specs=[pl.BlockSpec((None, m, d), lambda b: (b, 0, 0)),
                  pl.BlockSpec((1, d), lambda b: (0, 0)),
                  pl.BlockSpec((d, cols), lambda b: (0, 0))],
        out_specs=pl.BlockSpec((None, m, cols), lambda b: (b, 0, 0)),
        out_shape=jax.ShapeDtypeStruct((bsz, m, cols), BF16),
        compiler_params=_params(("parallel",)),
        name="mem_kv",
    )(mem, g, w)


def _mem_attn_body(u_ref, wq_ref, kv_ref, o_ref, *, n_heads):
    width = n_heads * HEAD_DIM
    scale = HEAD_DIM ** -0.5
    q = (jnp.dot(u_ref[...], wq_ref[...], preferred_element_type=F32) * scale).astype(BF16)
    for h in range(n_heads):
        lo, hi = h * HEAD_DIM, (h + 1) * HEAD_DIM
        s = _dot_nt(q[:, lo:hi], kv_ref[:, lo:hi])
        p = jnp.exp(s - jnp.max(s, axis=1, keepdims=True))
        p = p / jnp.sum(p, axis=1, keepdims=True)
        o = jnp.dot(p.astype(BF16), kv_ref[:, width + lo:width + hi], preferred_element_type=F32)
        o_ref[:, lo:hi] = o.astype(o_ref.dtype)


def _mem_attn(u, w_mq, mem_kv, n_heads, tq):
    bsz, seq, d = u.shape
    width = n_heads * HEAD_DIM
    m = mem_kv.shape[1]
    return pl.pallas_call(
        functools.partial(_mem_attn_body, n_heads=n_heads),
        grid=(bsz, seq // tq),
        in_specs=[pl.BlockSpec((None, tq, d), lambda b, i: (b, i, 0)),
                  pl.BlockSpec((d, width), lambda b, i: (0, 0)),
                  pl.BlockSpec((None, m, 2 * width), lambda b, i: (b, 0, 0))],
        out_specs=pl.BlockSpec((None, tq, width), lambda b, i: (b, i, 0)),
        out_shape=jax.ShapeDtypeStruct((bsz, seq, width), BF16),
        compiler_params=_params(("parallel", "parallel")),
        name="mem_attn",
    )(u, w_mq, mem_kv)


def _merge_body(oh_ref, of_ref, om_ref, g0_ref, g1_ref, g2_ref, x_ref, post_ref,
                wh_ref, wf_ref, wm_ref, wo_ref, out_ref):
    merged = g0_ref[...].astype(F32) * jnp.dot(oh_ref[...], wh_ref[...], preferred_element_type=F32)
    merged += g1_ref[...].astype(F32) * jnp.dot(of_ref[...], wf_ref[...], preferred_element_type=F32)
    merged += g2_ref[...].astype(F32) * jnp.dot(om_ref[...], wm_ref[...], preferred_element_type=F32)
    y = jnp.dot(merged.astype(BF16), wo_ref[...], preferred_element_type=F32)
    out_ref[...] = x_ref[...] + _rms(y, post_ref[...])


def _merge(o_h, o_f, o_m, gates, x1, post, wh, wf, wm, wo, tm):
    n, d = x1.shape
    row = lambda width: pl.BlockSpec((tm, width), lambda i: (i, 0))
    gate = lambda k: pl.BlockSpec((tm, d), lambda i, k=k: (i, k))
    full = lambda w: pl.BlockSpec(w.shape, lambda i: (0, 0), pipeline_mode=pl.Buffered(1))
    return pl.pallas_call(
        _merge_body,
        grid=(n // tm,),
        in_specs=[row(o_h.shape[1]), row(o_f.shape[1]), row(o_m.shape[1]),
                  gate(0), gate(1), gate(2), row(d),
                  pl.BlockSpec((1, d), lambda i: (0, 0)),
                  full(wh), full(wf), full(wm), full(wo)],
        out_specs=row(d),
        out_shape=jax.ShapeDtypeStruct((n, d), F32),
        compiler_params=_params(("parallel",)),
        name="merge",
    )(o_h, o_f, o_m, gates, gates, gates, x1, post, wh, wf, wm, wo)


def _pad_cols(w, to):
    return jnp.pad(w, ((0, 0), (0, to - w.shape[1])))


def _ffn_weights(wg, wu, wd):
    return wg.astype(BF16), wu.astype(BF16), wd.astype(BF16)


def kernel(x, mem, ffn1_pre, ffn1_post, ffn1_wg, ffn1_wu, ffn1_wd, mix_pre, mix_post, mem_norm,
           w_in, hgrn_lb, hgrn_gnorm, fox_fb, w_mem_kv, w_hgrn_out, w_fox_out, w_mem_out, w_gate,
           w_o, ffn2_pre, ffn2_post, ffn2_wg, ffn2_wu, ffn2_wd):
    bsz, seq, d = x.shape
    n = bsz * seq
    depth = ffn1_pre.shape[0]
    wh, wf, wm = w_hgrn_out.shape[1], w_fox_out.shape[1], w_mem_out.shape[1]
    h_hgrn, h_fox, h_mem = wh // HEAD_DIM, wf // HEAD_DIM, wm // HEAD_DIM
    c_fox = 4 * wh
    c_ff = c_fox + 3 * wf
    c_mq = c_ff + h_fox
    n_gate = w_gate.shape[2]
    assert h_fox <= LANES and n_gate == 3 * d
    blk = lambda c: c // HEAD_DIM
    t = _tiles(n, seq, c_ff, n_gate, h_fox, (blk(c_fox), blk(c_fox + wf)))
    scale = np.ones((1, c_ff), np.float32)
    scale[:, c_fox:c_fox + wf] = HEAD_DIM ** -0.5 * LOG2E
    scale = jnp.asarray(scale)
    no_scale = jnp.ones((1, n_gate), F32)

    x2 = x.reshape(n, d)
    vec = lambda a: a.reshape(1, -1)
    for l in range(depth):
        wg1, wu1, wd1 = _ffn_weights(ffn1_wg[l], ffn1_wu[l], ffn1_wd[l])
        x1, u = _ffn(x2, vec(ffn1_pre[l]), vec(ffn1_post[l]), vec(mix_pre[l]), wg1, wu1, wd1,
                     t["ffn_tm"], t["ffn_tf"])
        u3 = u.reshape(bsz, seq, d)

        proj = _proj(u, w_in, l, c_ff, scale, False, t["proj_tm"], t["proj_tn"], "in_proj")
        gates = _proj(u, w_gate, l, n_gate, no_scale, True, t["proj_tm"], t["gate_tn"], "gate_proj")
        proj3 = proj.reshape(bsz, seq, -1)

        o_h = _hgrn(proj3, hgrn_lb, l, vec(hgrn_gnorm[l]), h_hgrn, 0, t["hgrn_chunk"], t["hgrn_tt"])

        w_tail = w_in[l][:, c_ff:]
        w_ff = _pad_cols(w_tail[:, :h_fox], LANES).astype(BF16)
        qa, ka, vt = _fox_prep(u3, w_ff, _pad_cols(vec(fox_fb[l]), LANES), proj3,
                               blk(c_fox + 2 * wf), h_fox, t["prep_tt"])
        o_f = _fox(proj3, qa, ka, vt, h_fox, blk(c_fox), blk(c_fox + wf), t["fox_t"], t["fox_group"])

        mkv = _mem_kv(mem, vec(mem_norm[l]), w_mem_kv[l].astype(BF16))
        o_m = _mem_attn(u3, w_tail[:, h_fox:].astype(BF16), mkv, h_mem, t["mem_tq"])

        x2 = _merge(o_h.reshape(n, wh), o_f.reshape(n, wf), o_m.reshape(n, wm), gates,
                    x1, vec(mix_post[l]), w_hgrn_out[l].astype(BF16), w_fox_out[l].astype(BF16),
                    w_mem_out[l].astype(BF16), w_o[l].astype(BF16), t["merge_tm"])

        wg2, wu2, wd2 = _ffn_weights(ffn2_wg[l], ffn2_wu[l], ffn2_wd[l])
        x2 = _ffn(x2, vec(ffn2_pre[l]), vec(ffn2_post[l]), None, wg2, wu2, wd2,
                  t["ffn_tm"], t["ffn_tf"])
    return x2.reshape(bsz, seq, d)
```

```python
import functools
import math

import numpy as np
import jax
import jax.numpy as jnp
from jax import lax
from jax.experimental import pallas as pl
from jax.experimental.pallas import tpu as pltpu

F32 = jnp.float32
BF16 = jnp.bfloat16
EPS = 1e-6
HEAD_DIM = 128
LANES = 128
LOG2E = math.log2(math.e)
VMEM_LIMIT_BYTES = 56 * 1024 * 1024
NEG_BIG = -1e30


def _tiles(n_tokens, seq, proj_cols, gate_cols, fox_heads, fox_head_cols):
    widest = lambda cols: next(w for w in (1024, 768, 512, 256, 128) if cols % w == 0)
    fox_group = next(g for g in (6, 3, 2, 1)
                     if fox_heads % g == 0 and all(c % g == 0 for c in fox_head_cols))
    return dict(
        ffn_tm=min(512, n_tokens), ffn_tf=1024,
        proj_tm=min(1024, n_tokens), proj_tn=widest(proj_cols), gate_tn=widest(gate_cols),
        hgrn_chunk=min(128, seq), hgrn_tt=min(1024, seq),
        prep_tt=min(256, seq),
        fox_t=min(512, seq), fox_group=fox_group,
        mem_tq=min(512, seq),
        merge_tm=min(512, n_tokens),
    )


def _params(sem):
    return pltpu.CompilerParams(dimension_semantics=sem, vmem_limit_bytes=VMEM_LIMIT_BYTES)


def _rms(x, g):
    ms = jnp.mean(x * x, axis=-1, keepdims=True)
    return x * lax.rsqrt(ms + EPS) * g


def _sigmoid(x):
    return 0.5 * jnp.tanh(0.5 * x) + 0.5


def _silu(x):
    return x * _sigmoid(x)


def _dot_nt(a, b):
    return lax.dot_general(a, b, (((1,), (1,)), ((), ())), preferred_element_type=F32)


def _dot_tn(a, b):
    return lax.dot_general(a, b, (((0,), (0,)), ((), ())), preferred_element_type=F32)


def _ffn_body(*refs, emit_next, ragged_width, n_tiles):
    if emit_next:
        x_ref, pre_ref, post_ref, nxt_ref, wg_ref, wu_ref, wd_ref, out_ref, u_ref, xn_s, acc_s = refs
    else:
        x_ref, pre_ref, post_ref, wg_ref, wu_ref, wd_ref, out_ref, xn_s, acc_s = refs
    j = pl.program_id(1)
    last = pl.num_programs(1) - 1
    full_width = wg_ref.shape[1]

    def hidden_tile(width):
        xn = xn_s[...]
        g = jnp.dot(xn, wg_ref[:, :width], preferred_element_type=F32)
        u = jnp.dot(xn, wu_ref[:, :width], preferred_element_type=F32)
        h = (_silu(g) * u).astype(BF16)
        return jnp.dot(h, wd_ref[:width, :], preferred_element_type=F32)

    @pl.when(j == 0)
    def _():
        xn_s[...] = _rms(x_ref[...], pre_ref[...]).astype(BF16)
        acc_s[...] = hidden_tile(ragged_width)

    @pl.when(jnp.logical_and(j > 0, j < last))
    def _():
        acc_s[...] += hidden_tile(full_width)

    @pl.when(j == last)
    def _():
        if n_tiles > 1:
            acc_s[...] += hidden_tile(full_width)
        x1 = x_ref[...] + 0.5 * _rms(acc_s[...], post_ref[...])
        out_ref[...] = x1
        if emit_next:
            u_ref[...] = _rms(x1, nxt_ref[...]).astype(BF16)


def _ffn(x, pre, post, nxt, wg, wu, wd, tm, tf):
    n, d = x.shape
    f = wg.shape[1]
    n_f = pl.cdiv(f, tf)
    emit_next = nxt is not None
    row = lambda i, j: (i, 0)
    vec = pl.BlockSpec((1, d), lambda i, j: (0, 0))
    in_specs = [pl.BlockSpec((tm, d), row), vec, vec]
    args = [x, pre, post]
    if emit_next:
        in_specs.append(vec)
        args.append(nxt)
    tile = lambda j: (j + n_f - 1) % n_f
    in_specs += [pl.BlockSpec((d, tf), lambda i, j: (0, tile(j))),
                 pl.BlockSpec((d, tf), lambda i, j: (0, tile(j))),
                 pl.BlockSpec((tf, d), lambda i, j: (tile(j), 0))]
    args += [wg, wu, wd]
    out_shape = [jax.ShapeDtypeStruct((n, d), F32)]
    out_specs = [pl.BlockSpec((tm, d), row)]
    if emit_next:
        out_shape.append(jax.ShapeDtypeStruct((n, d), BF16))
        out_specs.append(pl.BlockSpec((tm, d), row))
    res = pl.pallas_call(
        functools.partial(_ffn_body, emit_next=emit_next, ragged_width=f - (n_f - 1) * tf,
                          n_tiles=n_f),
        grid=(n // tm, n_f),
        in_specs=in_specs, out_specs=out_specs, out_shape=out_shape,
        scratch_shapes=[pltpu.VMEM((tm, d), BF16), pltpu.VMEM((tm, d), F32)],
        compiler_params=_params(("parallel", "arbitrary")),
        name="ffn_next" if emit_next else "ffn",
    )(*args)
    return res if emit_next else res[0]


def _proj_body(u_ref, w_ref, s_ref, out_ref, wb_s, *, gate):
    @pl.when(pl.program_id(1) == 0)
    def _():
        wb_s[...] = w_ref[...].astype(BF16)

    acc = jnp.dot(u_ref[...], wb_s[...], preferred_element_type=F32)
    if gate:
        out_ref[...] = _sigmoid(acc).astype(out_ref.dtype)
    else:
        out_ref[...] = (acc * s_ref[...]).astype(out_ref.dtype)


def _proj(u, w, layer, cols, scale, gate, tm, tn, name):
    n, d = u.shape
    assert cols % tn == 0
    return pl.pallas_call(
        functools.partial(_proj_body, gate=gate),
        grid=(cols // tn, n // tm),
        in_specs=[pl.BlockSpec((tm, d), lambda j, i: (i, 0)),
                  pl.BlockSpec((None, d, tn), lambda j, i: (layer, 0, j)),
                  pl.BlockSpec((1, tn), lambda j, i: (0, j))],
        out_specs=pl.BlockSpec((tm, tn), lambda j, i: (i, j)),
        out_shape=jax.ShapeDtypeStruct((n, cols), BF16),
        scratch_shapes=[pltpu.VMEM((d, tn), BF16)],
        compiler_params=_params(("parallel", "arbitrary")),
        name=name,
    )(u, w, scale)


def _hgrn_levels(chunk):
    return [chunk >> (i + 1) for i in range(int(math.log2(chunk)))]


def _hgrn_constants(chunk):
    t = np.arange(chunk)[:, None]
    r = np.arange(chunk)[None, :]
    blocks = [(r <= t), (r > t)]
    masks = [(r == t)]
    for h in _hgrn_levels(chunk):
        ref = (t // (2 * h)) * (2 * h) + h - 1
        upper = (t & h) != 0
        blocks.append(np.where(upper, (r > ref) & (r <= t), (r > t) & (r <= ref)))
        masks.append(((t ^ r) >= h) & ((t ^ r) < 2 * h) & upper)
    a = np.concatenate(blocks, axis=0).astype(np.float32)
    m = np.stack(masks, axis=0).astype(np.float32)
    return jnp.asarray(np.concatenate([a, a], axis=1), BF16), jnp.asarray(m, F32)


def _hgrn_body(lb_ref, gn_ref, a_ref, mask_ref, hq_ref, hf_ref, hi_ref, hog_ref, o_ref, st_s, ex_s,
               *, chunk, n_chunks, layer):
    @pl.when(pl.program_id(2) == 0)
    def _():
        st_s[...] = jnp.zeros_like(st_s)

    slots = [lb_ref[r:r + 1, :] for r in range(lb_ref.shape[0])]
    top = functools.reduce(jnp.maximum, slots)
    es = [jnp.exp(s - top) for s in slots]
    lb = sum(es[1:layer + 1], es[0]) / sum(es[1:], es[0])
    gn = gn_ref[...]
    levels = _hgrn_levels(chunk)
    row = lax.broadcasted_iota(jnp.int32, (chunk, HEAD_DIM), 0)
    rows = [slice(c * chunk, (c + 1) * chunk) for c in range(n_chunks)]

    g = lb + (1.0 - lb) * _sigmoid(hf_ref[...].astype(F32))
    lg = jnp.log(g)
    lg_hi = lg.astype(BF16)
    lg_lo = (lg - lg_hi.astype(F32)).astype(BF16)
    pieces = jnp.concatenate([jnp.concatenate([p[r] for r in rows], axis=1) for p in (lg_hi, lg_lo)],
                             axis=0)
    ex_s[...] = jnp.exp(jnp.dot(a_ref[...], pieces, preferred_element_type=F32))

    def intra(c):
        r, ln = rows[c], slice(c * HEAD_DIM, (c + 1) * HEAD_DIM)
        kk = 1.0 - g[r]
        q = _silu(hq_ref[r, :].astype(F32))
        scores = _dot_nt(q.astype(BF16), kk.astype(BF16)) * mask_ref[0]
        for li, h in enumerate(levels):
            f = ex_s[(2 + li) * chunk:(3 + li) * chunk, ln]
            y = (jnp.where((row & h) != 0, q, kk) * f).astype(BF16)
            scores = scores + _dot_nt(y, y) * mask_ref[1 + li]
        o_intra = jnp.dot(scores.astype(BF16), hi_ref[r, :], preferred_element_type=F32)
        q_in = (q * ex_s[0:chunk, ln]).astype(BF16)
        k_end = (kk * ex_s[chunk:2 * chunk, ln]).astype(BF16)
        return o_intra, q_in, k_end

    def finish(c, o_intra, q_in, k_end):
        r, ln = rows[c], slice(c * HEAD_DIM, (c + 1) * HEAD_DIM)
        st = st_s[...]
        o = o_intra + _dot_nt(q_in, st.astype(BF16))
        st_s[...] = st * ex_s[chunk - 1:chunk, ln] + _dot_tn(hi_ref[r, :], k_end)
        o = _rms(o, gn) * _silu(hog_ref[r, :].astype(F32))
        o_ref[r, :] = o.astype(o_ref.dtype)

    ahead = intra(0)
    for c in range(n_chunks):
        cur, ahead = ahead, (intra(c + 1) if c + 1 < n_chunks else None)
        finish(c, *cur)


def _hgrn(proj, hgrn_lb, layer, gnorm, n_heads, col0, chunk, tt):
    bsz, seq, _ = proj.shape
    a_mat, masks = _hgrn_constants(chunk)
    n_slots = hgrn_lb.shape[0]

    def col(k):
        return pl.BlockSpec((None, tt, HEAD_DIM), lambda b, h, t, k=k: (b, t, col0 + k * n_heads + h))

    return pl.pallas_call(
        functools.partial(_hgrn_body, chunk=chunk, n_chunks=tt // chunk, layer=layer),
        grid=(bsz, n_heads, seq // tt),
        in_specs=[pl.BlockSpec((n_slots, HEAD_DIM), lambda b, h, t: (0, h)),
                  pl.BlockSpec((1, HEAD_DIM), lambda b, h, t: (0, h)),
                  pl.BlockSpec(a_mat.shape, lambda b, h, t: (0, 0)),
                  pl.BlockSpec(masks.shape, lambda b, h, t: (0, 0, 0)),
                  col(0), col(1), col(2), col(3)],
        out_specs=pl.BlockSpec((None, tt, HEAD_DIM), lambda b, h, t: (b, t, h)),
        out_shape=jax.ShapeDtypeStruct((bsz, seq, n_heads * HEAD_DIM), BF16),
        scratch_shapes=[pltpu.VMEM((HEAD_DIM, HEAD_DIM), F32),
                        pltpu.VMEM((a_mat.shape[0], (tt // chunk) * HEAD_DIM), F32)],
        compiler_params=_params(("parallel", "parallel", "arbitrary")),
        name="hgrn",
    )(hgrn_lb, gnorm, a_mat, masks, proj, proj, proj, proj)


def _split3(x):
    hi = x.astype(BF16).astype(F32)
    r = x - hi
    mid = r.astype(BF16).astype(F32)
    lo = (r - mid).astype(BF16).astype(F32)
    return hi, mid, lo


V_ROWS = HEAD_DIM + 16


def _fox_bias_placement(n_heads):
    width = n_heads * LANES
    assert 6 * n_heads <= LANES
    place = np.zeros((3 * LANES, width + LANES), np.float32)
    const = np.zeros((1, width + LANES), np.float32)
    for h in range(n_heads):
        for p in range(3):
            place[p * LANES + h, h * LANES + 6 * h + p] = 1.0
            place[p * LANES + h, width + 6 * h + 3 + p] = -1.0
        const[0, h * LANES + 6 * h + 3:h * LANES + 6 * h + 6] = 1.0
        const[0, width + 6 * h:width + 6 * h + 3] = 1.0
    return jnp.asarray(place, BF16), jnp.asarray(const, F32)


def _fox_prep_body(u_ref, wff_ref, fb_ref, tri_ref, place_ref, const_ref, v_ref, qa_ref, ka_ref, vt_ref,
                   carry_s, *, n_heads):
    @pl.when(pl.program_id(1) == 0)
    def _():
        carry_s[...] = jnp.zeros_like(carry_s)

    tt = u_ref.shape[0]
    width = n_heads * LANES
    z = jnp.dot(u_ref[...], wff_ref[...], preferred_element_type=F32) + fb_ref[...]
    log_f = jnp.minimum(z, 0.0) - jnp.log(1.0 + jnp.exp(-jnp.abs(z)))
    tri = tri_ref[...]
    c = carry_s[...]
    for piece in _split3(log_f):
        c = c + jnp.dot(tri, piece.astype(BF16), preferred_element_type=F32)
    carry_s[...] = c[tt - 1:tt, :]
    pieces = jnp.concatenate(_split3(c * LOG2E), axis=1).astype(BF16)
    bias = jnp.dot(pieces, place_ref[...], preferred_element_type=F32) + const_ref[...]
    qa_ref[...] = bias[:, :width].astype(BF16)
    ka_ref[...] = bias[:, width:].astype(BF16)
    ones = jnp.ones((V_ROWS - HEAD_DIM, tt), BF16)
    for h in range(n_heads):
        vt_ref[h * V_ROWS:h * V_ROWS + HEAD_DIM, :] = (
            v_ref[:, h * LANES:(h + 1) * LANES].astype(F32).T.astype(BF16))
        vt_ref[h * V_ROWS + HEAD_DIM:(h + 1) * V_ROWS, :] = ones


def _fox_prep(u, w_ff, fox_b, proj, col_v, n_heads, tt):
    bsz, seq, d = u.shape
    width = n_heads * LANES
    assert col_v % n_heads == 0
    tri = jnp.asarray(np.tril(np.ones((tt, tt), np.float32)), BF16)
    place, const = _fox_bias_placement(n_heads)
    out = jax.ShapeDtypeStruct((bsz, seq, width), BF16)
    blk = pl.BlockSpec((None, tt, width), lambda b, t: (b, t, 0))
    whole = lambda a: pl.BlockSpec(a.shape, lambda b, t: (0, 0))
    return pl.pallas_call(
        functools.partial(_fox_prep_body, n_heads=n_heads),
        grid=(bsz, seq // tt),
        in_specs=[pl.BlockSpec((None, tt, d), lambda b, t: (b, t, 0)),
                  whole(w_ff), whole(fox_b), whole(tri), whole(place), whole(const),
                  pl.BlockSpec((None, tt, width), lambda b, t: (b, t, col_v // n_heads))],
        out_specs=[blk, pl.BlockSpec((None, tt, LANES), lambda b, t: (b, t, 0)),
                   pl.BlockSpec((None, n_heads * V_ROWS, tt), lambda b, t: (b, 0, t))],
        out_shape=[out, jax.ShapeDtypeStruct((bsz, seq, LANES), BF16),
                   jax.ShapeDtypeStruct((bsz, n_heads * V_ROWS, seq), BF16)],
        scratch_shapes=[pltpu.VMEM((1, LANES), F32)],
        compiler_params=_params(("parallel", "arbitrary")),
        name="fox_prep",
    )(u, w_ff, fox_b, tri, place, const, proj)


def _fox_body(q_ref, qa_ref, k_ref, ka_ref, vt_ref, o_ref, acc_s, st_s, *, tile, group):
    qi = pl.program_id(2)
    lanes = [slice(g * HEAD_DIM, (g + 1) * HEAD_DIM) for g in range(group)]
    qcs = [jnp.concatenate([q_ref[:, ln], qa_ref[:, ln]], axis=1) for ln in lanes]

    def logits(g, j):
        ks = pl.ds(pl.multiple_of(j * tile, tile), tile)
        kc = jnp.concatenate([k_ref[ks, lanes[g]], ka_ref[ks, :]], axis=1)
        return _dot_nt(kc, qcs[g])

    def accumulate(g, j, m, alpha):
        ks = pl.ds(pl.multiple_of(j * tile, tile), tile)
        pt = jnp.exp2(st_s[g] - m).astype(BF16)
        pv = jnp.dot(vt_ref[g * V_ROWS:(g + 1) * V_ROWS, ks], pt, preferred_element_type=F32)
        acc_s[g] = alpha * acc_s[g] + pv

    acc_s[...] = jnp.zeros_like(acc_s)
    key = lax.broadcasted_iota(jnp.int32, (tile, tile), 0)
    qry = lax.broadcasted_iota(jnp.int32, (tile, tile), 1)
    carry = []
    for g in range(group):
        st = jnp.where(key <= qry, logits(g, qi), NEG_BIG)
        st_s[g] = st
        carry.append((jnp.max(st, axis=0, keepdims=True), jnp.ones((1, tile), F32)))

    def step(i, carry):
        prev = jnp.where(i == 0, qi, i - 1)
        out = []
        for g in range(group):
            m, alpha = carry[g]
            st = logits(g, i)
            accumulate(g, prev, m, alpha)
            m_new = jnp.maximum(m, jnp.max(st, axis=0, keepdims=True))
            st_s[g] = st
            out.append((m_new, jnp.exp2(m - m_new)))
        return tuple(out)

    carry = lax.fori_loop(0, qi, step, tuple(carry))
    last = jnp.where(qi == 0, qi, qi - 1)
    for g, ln in enumerate(lanes):
        accumulate(g, last, *carry[g])
        acc = acc_s[g]
        o_ref[:, ln] = (acc[:HEAD_DIM] / acc[HEAD_DIM:HEAD_DIM + 1]).T.astype(o_ref.dtype)


def _fox(proj, qa, ka, vt, n_heads, col_q, col_k, tile, group):
    bsz, seq, _ = proj.shape
    width = group * HEAD_DIM
    assert n_heads % group == 0 and col_q % group == 0 and col_k % group == 0
    qblk = lambda c0: pl.BlockSpec((None, tile, width), lambda b, h, i: (b, i, c0 // group + h))
    kblk = lambda c0: pl.BlockSpec((None, seq, width), lambda b, h, i: (b, 0, c0 // group + h),
                                   pipeline_mode=pl.Buffered(1))
    return pl.pallas_call(
        functools.partial(_fox_body, tile=tile, group=group),
        grid=(bsz, n_heads // group, seq // tile),
        in_specs=[qblk(col_q), qblk(0), kblk(col_k),
                  pl.BlockSpec((None, seq, LANES), lambda b, h, i: (b, 0, 0), pipeline_mode=pl.Buffered(1)),
                  pl.BlockSpec((None, group * V_ROWS, seq), lambda b, h, i: (b, h, 0),
                               pipeline_mode=pl.Buffered(1))],
        out_specs=pl.BlockSpec((None, tile, width), lambda b, h, i: (b, i, h)),
        out_shape=jax.ShapeDtypeStruct((bsz, seq, n_heads * HEAD_DIM), BF16),
        scratch_shapes=[pltpu.VMEM((group, V_ROWS, tile), F32), pltpu.VMEM((group, tile, tile), F32)],
        compiler_params=_params(("parallel", "parallel", "arbitrary")),
        name="fox",
    )(proj, qa, proj, ka, vt)


def _mem_kv_body(mem_ref, g_ref, w_ref, out_ref):
    mn = _rms(mem_ref[...], g_ref[...]).astype(BF16)
    out_ref[...] = jnp.dot(mn, w_ref[...], preferred_element_type=F32).astype(out_ref.dtype)


def _mem_kv(mem, g, w):
    bsz, m, d = mem.shape
    cols = w.shape[1]
    return pl.pallas_call(
        _mem_kv_body,
        grid=(bsz,),
        in_specs=[pl.BlockSpec((None, m, d), lambda b: (b, 0, 0)),
                  pl.BlockSpec((1, d), lambda b: (0, 0)),
                  pl.BlockSpec((d, cols), lambda b: (0, 0))],
        out_specs=pl.BlockSpec((None, m, cols), lambda b: (b, 0, 0)),
        out_shape=jax.ShapeDtypeStruct((bsz, m, cols), BF16),
        compiler_params=_params(("parallel",)),
        name="mem_kv",
    )(mem, g, w)


def _mem_attn_body(u_ref, wq_ref, kv_ref, o_ref, *, n_heads):
    width = n_heads * HEAD_DIM
    scale = HEAD_DIM ** -0.5
    q = (jnp.dot(u_ref[...], wq_ref[...], preferred_element_type=F32) * scale).astype(BF16)
    heads = [slice(h * HEAD_DIM, (h + 1) * HEAD_DIM) for h in range(n_heads)]
    scores = [_dot_nt(q[:, hd], kv_ref[:, hd]) for hd in heads]
    for hd, s in zip(heads, scores):
        p = jnp.exp(s - jnp.max(s, axis=1, keepdims=True))
        p = p / jnp.sum(p, axis=1, keepdims=True)
        o = jnp.dot(p.astype(BF16), kv_ref[:, width + hd.start:width + hd.stop],
                    preferred_element_type=F32)
        o_ref[:, hd] = o.astype(o_ref.dtype)


def _mem_attn(u, w_mq, mem_kv, n_heads, tq):
    bsz, seq, d = u.shape
    width = n_heads * HEAD_DIM
    m = mem_kv.shape[1]
    return pl.pallas_call(
        functools.partial(_mem_attn_body, n_heads=n_heads),
        grid=(bsz, seq // tq),
        in_specs=[pl.BlockSpec((None, tq, d), lambda b, i: (b, i, 0)),
                  pl.BlockSpec((d, width), lambda b, i: (0, 0)),
                  pl.BlockSpec((None, m, 2 * width), lambda b, i: (b, 0, 0))],
        out_specs=pl.BlockSpec((None, tq, width), lambda b, i: (b, i, 0)),
        out_shape=jax.ShapeDtypeStruct((bsz, seq, width), BF16),
        compiler_params=_params(("parallel", "parallel")),
        name="mem_attn",
    )(u, w_mq, mem_kv)


def _merge_body(oh_ref, of_ref, om_ref, g0_ref, g1_ref, g2_ref, x_ref, post_ref,
                wh_ref, wf_ref, wm_ref, wo_ref, out_ref):
    merged = g0_ref[...].astype(F32) * jnp.dot(oh_ref[...], wh_ref[...], preferred_element_type=F32)
    merged += g1_ref[...].astype(F32) * jnp.dot(of_ref[...], wf_ref[...], preferred_element_type=F32)
    merged += g2_ref[...].astype(F32) * jnp.dot(om_ref[...], wm_ref[...], preferred_element_type=F32)
    y = jnp.dot(merged.astype(BF16), wo_ref[...], preferred_element_type=F32)
    out_ref[...] = x_ref[...] + _rms(y, post_ref[...])


def _merge(o_h, o_f, o_m, gates, x1, post, wh, wf, wm, wo, tm):
    n, d = x1.shape
    row = lambda width: pl.BlockSpec((tm, width), lambda i: (i, 0))
    gate = lambda k: pl.BlockSpec((tm, d), lambda i, k=k: (i, k))
    full = lambda w: pl.BlockSpec(w.shape, lambda i: (0, 0), pipeline_mode=pl.Buffered(1))
    return pl.pallas_call(
        _merge_body,
        grid=(n // tm,),
        in_specs=[row(o_h.shape[1]), row(o_f.shape[1]), row(o_m.shape[1]),
                  gate(0), gate(1), gate(2), row(d),
                  pl.BlockSpec((1, d), lambda i: (0, 0)),
                  full(wh), full(wf), full(wm), full(wo)],
        out_specs=row(d),
        out_shape=jax.ShapeDtypeStruct((n, d), F32),
        compiler_params=_params(("parallel",)),
        name="merge",
    )(o_h, o_f, o_m, gates, gates, gates, x1, post, wh, wf, wm, wo)


def _pad_cols(w, to):
    return jnp.pad(w, ((0, 0), (0, to - w.shape[1])))


def _ffn_weights(wg, wu, wd):
    return wg.astype(BF16), wu.astype(BF16), wd.astype(BF16)


def kernel(x, mem, ffn1_pre, ffn1_post, ffn1_wg, ffn1_wu, ffn1_wd, mix_pre, mix_post, mem_norm,
           w_in, hgrn_lb, hgrn_gnorm, fox_fb, w_mem_kv, w_hgrn_out, w_fox_out, w_mem_out, w_gate,
           w_o, ffn2_pre, ffn2_post, ffn2_wg, ffn2_wu, ffn2_wd):
    bsz, seq, d = x.shape
    n = bsz * seq
    depth = ffn1_pre.shape[0]
    wh, wf, wm = w_hgrn_out.shape[1], w_fox_out.shape[1], w_mem_out.shape[1]
    h_hgrn, h_fox, h_mem = wh // HEAD_DIM, wf // HEAD_DIM, wm // HEAD_DIM
    c_fox = 4 * wh
    c_ff = c_fox + 3 * wf
    c_mq = c_ff + h_fox
    n_gate = w_gate.shape[2]
    assert h_fox <= LANES and n_gate == 3 * d
    blk = lambda c: c // HEAD_DIM
    t = _tiles(n, seq, c_ff, n_gate, h_fox, (blk(c_fox), blk(c_fox + wf)))
    scale = np.ones((1, c_ff), np.float32)
    scale[:, c_fox:c_fox + wf] = HEAD_DIM ** -0.5 * LOG2E
    scale = jnp.asarray(scale)
    no_scale = jnp.ones((1, n_gate), F32)

    x2 = x.reshape(n, d)
    vec = lambda a: a.reshape(1, -1)
    for l in range(depth):
        wg1, wu1, wd1 = _ffn_weights(ffn1_wg[l], ffn1_wu[l], ffn1_wd[l])
        x1, u = _ffn(x2, vec(ffn1_pre[l]), vec(ffn1_post[l]), vec(mix_pre[l]), wg1, wu1, wd1,
                     t["ffn_tm"], t["ffn_tf"])
        u3 = u.reshape(bsz, seq, d)

        proj = _proj(u, w_in, l, c_ff, scale, False, t["proj_tm"], t["proj_tn"], "in_proj")
        gates = _proj(u, w_gate, l, n_gate, no_scale, True, t["proj_tm"], t["gate_tn"], "gate_proj")
        proj3 = proj.reshape(bsz, seq, -1)

        o_h = _hgrn(proj3, hgrn_lb, l, vec(hgrn_gnorm[l]), h_hgrn, 0, t["hgrn_chunk"], t["hgrn_tt"])

        w_tail = w_in[l][:, c_ff:]
        w_ff = _pad_cols(w_tail[:, :h_fox], LANES).astype(BF16)
        qa, ka, vt = _fox_prep(u3, w_ff, _pad_cols(vec(fox_fb[l]), LANES), proj3,
                               blk(c_fox + 2 * wf), h_fox, t["prep_tt"])
        o_f = _fox(proj3, qa, ka, vt, h_fox, blk(c_fox), blk(c_fox + wf), t["fox_t"], t["fox_group"])

        mkv = _mem_kv(mem, vec(mem_norm[l]), w_mem_kv[l].astype(BF16))
        o_m = _mem_attn(u3, w_tail[:, h_fox:].astype(BF16), mkv, h_mem, t["mem_tq"])

        x2 = _merge(o_h.reshape(n, wh), o_f.reshape(n, wf), o_m.reshape(n, wm), gates,
                    x1, vec(mix_post[l]), w_hgrn_out[l].astype(BF16), w_fox_out[l].astype(BF16),
                    w_mem_out[l].astype(BF16), w_o[l].astype(BF16), t["merge_tm"])

        wg2, wu2, wd2 = _ffn_weights(ffn2_wg[l], ffn2_wu[l], ffn2_wd[l])
        x2 = _ffn(x2, vec(ffn2_pre[l]), vec(ffn2_post[l]), None, wg2, wu2, wd2,
                  t["ffn_tm"], t["ffn_tf"])
    return x2.reshape(bsz, seq, d)
```

```python
import functools
import math

import numpy as np
import jax
import jax.numpy as jnp
from jax import lax
from jax.experimental import pallas as pl
from jax.experimental.pallas import tpu as pltpu

F32 = jnp.float32
BF16 = jnp.bfloat16
EPS = 1e-6
HEAD_DIM = 128
LANES = 128
LOG2E = math.log2(math.e)
VMEM_LIMIT_BYTES = 56 * 1024 * 1024
NEG_BIG = -1e30


def _tiles(n_tokens, seq, proj_cols, gate_cols, fox_heads, fox_head_cols):
    widest = lambda cols: next(w for w in (1024, 768, 512, 256, 128) if cols % w == 0)
    fox_group = next(g for g in (6, 3, 2, 1)
                     if fox_heads % g == 0 and all(c % g == 0 for c in fox_head_cols))
    return dict(
        ffn_tm=min(512, n_tokens), ffn_tf=1024,
        proj_tm=min(1024, n_tokens), proj_tn=widest(proj_cols), gate_tn=widest(gate_cols),
        hgrn_chunk=min(128, seq), hgrn_tt=min(2048, seq),
        prep_tt=min(256, seq),
        fox_t=min(512, seq), fox_group=fox_group,
        mem_tq=min(512, seq),
        merge_tm=min(512, n_tokens),
    )


def _params(sem):
    return pltpu.CompilerParams(dimension_semantics=sem, vmem_limit_bytes=VMEM_LIMIT_BYTES)


def _rms(x, g):
    ms = jnp.mean(x * x, axis=-1, keepdims=True)
    return x * lax.rsqrt(ms + EPS) * g


def _sigmoid(x):
    return 0.5 * jnp.tanh(0.5 * x) + 0.5


def _silu(x):
    return x * _sigmoid(x)


def _dot_nt(a, b):
    return lax.dot_general(a, b, (((1,), (1,)), ((), ())), preferred_element_type=F32)


def _dot_tn(a, b):
    return lax.dot_general(a, b, (((0,), (0,)), ((), ())), preferred_element_type=F32)


def _ffn_body(*refs, emit_next, ragged_width, n_tiles):
    if emit_next:
        x_ref, pre_ref, post_ref, nxt_ref, wg_ref, wu_ref, wd_ref, out_ref, u_ref, xn_s, acc_s = refs
    else:
        x_ref, pre_ref, post_ref, wg_ref, wu_ref, wd_ref, out_ref, xn_s, acc_s = refs
    j = pl.program_id(1)
    last = pl.num_programs(1) - 1
    full_width = wg_ref.shape[1]

    def hidden_tile(width):
        xn = xn_s[...]
        g = jnp.dot(xn, wg_ref[:, :width], preferred_element_type=F32)
        u = jnp.dot(xn, wu_ref[:, :width], preferred_element_type=F32)
        h = (_silu(g) * u).astype(BF16)
        return jnp.dot(h, wd_ref[:width, :], preferred_element_type=F32)

    @pl.when(j == 0)
    def _():
        xn_s[...] = _rms(x_ref[...], pre_ref[...]).astype(BF16)
        acc_s[...] = hidden_tile(ragged_width)

    @pl.when(jnp.logical_and(j > 0, j < last))
    def _():
        acc_s[...] += hidden_tile(full_width)

    @pl.when(j == last)
    def _():
        if n_tiles > 1:
            acc_s[...] += hidden_tile(full_width)
        x1 = x_ref[...] + 0.5 * _rms(acc_s[...], post_ref[...])
        out_ref[...] = x1
        if emit_next:
            u_ref[...] = _rms(x1, nxt_ref[...]).astype(BF16)


def _ffn(x, pre, post, nxt, wg, wu, wd, tm, tf):
    n, d = x.shape
    f = wg.shape[1]
    n_f = pl.cdiv(f, tf)
    emit_next = nxt is not None
    row = lambda i, j: (i, 0)
    vec = pl.BlockSpec((1, d), lambda i, j: (0, 0))
    in_specs = [pl.BlockSpec((tm, d), row), vec, vec]
    args = [x, pre, post]
    if emit_next:
        in_specs.append(vec)
        args.append(nxt)
    tile = lambda j: (j + n_f - 1) % n_f
    in_specs += [pl.BlockSpec((d, tf), lambda i, j: (0, tile(j))),
                 pl.BlockSpec((d, tf), lambda i, j: (0, tile(j))),
                 pl.BlockSpec((tf, d), lambda i, j: (tile(j), 0))]
    args += [wg, wu, wd]
    out_shape = [jax.ShapeDtypeStruct((n, d), F32)]
    out_specs = [pl.BlockSpec((tm, d), row)]
    if emit_next:
        out_shape.append(jax.ShapeDtypeStruct((n, d), BF16))
        out_specs.append(pl.BlockSpec((tm, d), row))
    res = pl.pallas_call(
        functools.partial(_ffn_body, emit_next=emit_next, ragged_width=f - (n_f - 1) * tf,
                          n_tiles=n_f),
        grid=(n // tm, n_f),
        in_specs=in_specs, out_specs=out_specs, out_shape=out_shape,
        scratch_shapes=[pltpu.VMEM((tm, d), BF16), pltpu.VMEM((tm, d), F32)],
        compiler_params=_params(("parallel", "arbitrary")),
        name="ffn_next" if emit_next else "ffn",
    )(*args)
    return res if emit_next else res[0]


def _proj_body(u_ref, w_ref, s_ref, out_ref, wb_s, *, gate):
    @pl.when(pl.program_id(1) == 0)
    def _():
        wb_s[...] = w_ref[...].astype(BF16)

    acc = jnp.dot(u_ref[...], wb_s[...], preferred_element_type=F32)
    if gate:
        out_ref[...] = _sigmoid(acc).astype(out_ref.dtype)
    else:
        out_ref[...] = (acc * s_ref[...]).astype(out_ref.dtype)


def _proj(u, w, layer, cols, scale, gate, tm, tn, name):
    n, d = u.shape
    assert cols % tn == 0
    return pl.pallas_call(
        functools.partial(_proj_body, gate=gate),
        grid=(cols // tn, n // tm),
        in_specs=[pl.BlockSpec((tm, d), lambda j, i: (i, 0)),
                  pl.BlockSpec((None, d, tn), lambda j, i: (layer, 0, j)),
                  pl.BlockSpec((1, tn), lambda j, i: (0, j))],
        out_specs=pl.BlockSpec((tm, tn), lambda j, i: (i, j)),
        out_shape=jax.ShapeDtypeStruct((n, cols), BF16),
        scratch_shapes=[pltpu.VMEM((d, tn), BF16)],
        compiler_params=_params(("parallel", "arbitrary")),
        name=name,
    )(u, w, scale)


def _hgrn_levels(chunk):
    return [chunk >> (i + 1) for i in range(int(math.log2(chunk)))]


def _hgrn_constants(chunk):
    t = np.arange(chunk)[:, None]
    r = np.arange(chunk)[None, :]
    blocks = [(r <= t), (r > t)]
    masks = [(r == t)]
    for h in _hgrn_levels(chunk):
        ref = (t // (2 * h)) * (2 * h) + h - 1
        upper = (t & h) != 0
        blocks.append(np.where(upper, (r > ref) & (r <= t), (r > t) & (r <= ref)))
        masks.append(((t ^ r) >= h) & ((t ^ r) < 2 * h) & upper)
    a = np.concatenate(blocks, axis=0).astype(np.float32)
    m = np.stack(masks, axis=0).astype(np.float32)
    return jnp.asarray(np.concatenate([a, a], axis=1), BF16), jnp.asarray(m, F32)


def _hgrn_body(lb_ref, gn_ref, a_ref, mask_ref, hq_ref, hf_ref, hi_ref, hog_ref, o_ref, st_s, ex_s,
               *, chunk, n_chunks, layer):
    @pl.when(pl.program_id(2) == 0)
    def _():
        st_s[...] = jnp.zeros_like(st_s)

    slots = [lb_ref[r:r + 1, :] for r in range(lb_ref.shape[0])]
    top = functools.reduce(jnp.maximum, slots)
    es = [jnp.exp(s - top) for s in slots]
    lb = sum(es[1:layer + 1], es[0]) / sum(es[1:], es[0])
    gn = gn_ref[...]
    levels = _hgrn_levels(chunk)
    row = lax.broadcasted_iota(jnp.int32, (chunk, HEAD_DIM), 0)
    rows = [slice(c * chunk, (c + 1) * chunk) for c in range(n_chunks)]

    g = lb + (1.0 - lb) * _sigmoid(hf_ref[...].astype(F32))
    lg = jnp.log(g)
    lg_hi = lg.astype(BF16)
    lg_lo = (lg - lg_hi.astype(F32)).astype(BF16)
    pieces = jnp.concatenate([jnp.concatenate([p[r] for r in rows], axis=1) for p in (lg_hi, lg_lo)],
                             axis=0)
    ex_s[...] = jnp.exp(jnp.dot(a_ref[...], pieces, preferred_element_type=F32))

    def intra(c):
        r, ln = rows[c], slice(c * HEAD_DIM, (c + 1) * HEAD_DIM)
        kk = 1.0 - g[r]
        q = _silu(hq_ref[r, :].astype(F32))
        scores = _dot_nt(q.astype(BF16), kk.astype(BF16)) * mask_ref[0]
        for li, h in enumerate(levels):
            f = ex_s[(2 + li) * chunk:(3 + li) * chunk, ln]
            y = (jnp.where((row & h) != 0, q, kk) * f).astype(BF16)
            scores = scores + _dot_nt(y, y) * mask_ref[1 + li]
        o_intra = jnp.dot(scores.astype(BF16), hi_ref[r, :], preferred_element_type=F32)
        q_in = (q * ex_s[0:chunk, ln]).astype(BF16)
        k_end = (kk * ex_s[chunk:2 * chunk, ln]).astype(BF16)
        kv = _dot_tn(hi_ref[r, :], k_end)
        return o_intra, q_in, kv

    def finish(c, o_intra, q_in, kv):
        r, ln = rows[c], slice(c * HEAD_DIM, (c + 1) * HEAD_DIM)
        st = st_s[...]
        o = o_intra + _dot_nt(q_in, st.astype(BF16))
        st_s[...] = st * ex_s[chunk - 1:chunk, ln] + kv
        o = _rms(o, gn) * _silu(hog_ref[r, :].astype(F32))
        o_ref[r, :] = o.astype(o_ref.dtype)

    ahead = intra(0)
    for c in range(n_chunks):
        cur, ahead = ahead, (intra(c + 1) if c + 1 < n_chunks else None)
        finish(c, *cur)


def _hgrn(proj, hgrn_lb, layer, gnorm, n_heads, col0, chunk, tt):
    bsz, seq, _ = proj.shape
    a_mat, masks = _hgrn_constants(chunk)
    n_slots = hgrn_lb.shape[0]

    def col(k):
        return pl.BlockSpec((None, tt, HEAD_DIM), lambda b, h, t, k=k: (b, t, col0 + k * n_heads + h))

    return pl.pallas_call(
        functools.partial(_hgrn_body, chunk=chunk, n_chunks=tt // chunk, layer=layer),
        grid=(bsz, n_heads, seq // tt),
        in_specs=[pl.BlockSpec((n_slots, HEAD_DIM), lambda b, h, t: (0, h)),
                  pl.BlockSpec((1, HEAD_DIM), lambda b, h, t: (0, h)),
                  pl.BlockSpec(a_mat.shape, lambda b, h, t: (0, 0)),
                  pl.BlockSpec(masks.shape, lambda b, h, t: (0, 0, 0)),
                  col(0), col(1), col(2), col(3)],
        out_specs=pl.BlockSpec((None, tt, HEAD_DIM), lambda b, h, t: (b, t, h)),
        out_shape=jax.ShapeDtypeStruct((bsz, seq, n_heads * HEAD_DIM), BF16),
        scratch_shapes=[pltpu.VMEM((HEAD_DIM, HEAD_DIM), F32),
                        pltpu.VMEM((a_mat.shape[0], (tt // chunk) * HEAD_DIM), F32)],
        compiler_params=_params(("parallel", "parallel", "arbitrary")),
        name="hgrn",
    )(hgrn_lb, gnorm, a_mat, masks, proj, proj, proj, proj)


def _split3(x):
    hi = x.astype(BF16).astype(F32)
    r = x - hi
    mid = r.astype(BF16).astype(F32)
    lo = (r - mid).astype(BF16).astype(F32)
    return hi, mid, lo


V_ROWS = HEAD_DIM + 16


def _fox_bias_placement(n_heads):
    width = n_heads * LANES
    assert 6 * n_heads <= LANES
    place = np.zeros((3 * LANES, width + LANES), np.float32)
    const = np.zeros((1, width + LANES), np.float32)
    for h in range(n_heads):
        for p in range(3):
            place[p * LANES + h, h * LANES + 6 * h + p] = 1.0
            place[p * LANES + h, width + 6 * h + 3 + p] = -1.0
        const[0, h * LANES + 6 * h + 3:h * LANES + 6 * h + 6] = 1.0
        const[0, width + 6 * h:width + 6 * h + 3] = 1.0
    return jnp.asarray(place, BF16), jnp.asarray(const, F32)


def _fox_prep_body(u_ref, wff_ref, fb_ref, tri_ref, place_ref, const_ref, v_ref, qa_ref, ka_ref, vt_ref,
                   carry_s, *, n_heads):
    @pl.when(pl.program_id(1) == 0)
    def _():
        carry_s[...] = jnp.zeros_like(carry_s)

    tt = u_ref.shape[0]
    width = n_heads * LANES
    ones = jnp.ones((V_ROWS - HEAD_DIM, tt), BF16)
    for h in range(n_heads):
        vt_ref[h * V_ROWS:h * V_ROWS + HEAD_DIM, :] = (
            v_ref[:, h * LANES:(h + 1) * LANES].astype(F32).T.astype(BF16))
        vt_ref[h * V_ROWS + HEAD_DIM:(h + 1) * V_ROWS, :] = ones
    z = jnp.dot(u_ref[...], wff_ref[...], preferred_element_type=F32) + fb_ref[...]
    log_f = jnp.minimum(z, 0.0) - jnp.log(1.0 + jnp.exp(-jnp.abs(z)))
    tri = tri_ref[...]
    c = carry_s[...]
    for piece in _split3(log_f):
        c = c + jnp.dot(tri, piece.astype(BF16), preferred_element_type=F32)
    carry_s[...] = c[tt - 1:tt, :]
    pieces = jnp.concatenate(_split3(c * LOG2E), axis=1).astype(BF16)
    bias = jnp.dot(pieces, place_ref[...], preferred_element_type=F32) + const_ref[...]
    qa_ref[...] = bias[:, :width].astype(BF16)
    ka_ref[...] = bias[:, width:].astype(BF16)


def _fox_prep(u, w_ff, fox_b, proj, col_v, n_heads, tt):
    bsz, seq, d = u.shape
    width = n_heads * LANES
    assert col_v % n_heads == 0
    tri = jnp.asarray(np.tril(np.ones((tt, tt), np.float32)), BF16)
    place, const = _fox_bias_placement(n_heads)
    out = jax.ShapeDtypeStruct((bsz, seq, width), BF16)
    blk = pl.BlockSpec((None, tt, width), lambda b, t: (b, t, 0))
    whole = lambda a: pl.BlockSpec(a.shape, lambda b, t: (0, 0))
    return pl.pallas_call(
        functools.partial(_fox_prep_body, n_heads=n_heads),
        grid=(bsz, seq // tt),
        in_specs=[pl.BlockSpec((None, tt, d), lambda b, t: (b, t, 0)),
                  whole(w_ff), whole(fox_b), whole(tri), whole(place), whole(const),
                  pl.BlockSpec((None, tt, width), lambda b, t: (b, t, col_v // n_heads))],
        out_specs=[blk, pl.BlockSpec((None, tt, LANES), lambda b, t: (b, t, 0)),
                   pl.BlockSpec((None, n_heads * V_ROWS, tt), lambda b, t: (b, 0, t))],
        out_shape=[out, jax.ShapeDtypeStruct((bsz, seq, LANES), BF16),
                   jax.ShapeDtypeStruct((bsz, n_heads * V_ROWS, seq), BF16)],
        scratch_shapes=[pltpu.VMEM((1, LANES), F32)],
        compiler_params=_params(("parallel", "arbitrary")),
        name="fox_prep",
    )(u, w_ff, fox_b, tri, place, const, proj)


def _fox_body(q_ref, qa_ref, k_ref, ka_ref, vt_ref, o_ref, acc_s, st_s, *, tile, group):
    qi = pl.program_id(2)
    lanes = [slice(g * HEAD_DIM, (g + 1) * HEAD_DIM) for g in range(group)]
    qcs = [jnp.concatenate([q_ref[:, ln], qa_ref[:, ln]], axis=1) for ln in lanes]

    def logits(g, j):
        ks = pl.ds(pl.multiple_of(j * tile, tile), tile)
        kc = jnp.concatenate([k_ref[ks, lanes[g]], ka_ref[ks, :]], axis=1)
        return _dot_nt(kc, qcs[g])

    def accumulate(g, j, m, alpha):
        ks = pl.ds(pl.multiple_of(j * tile, tile), tile)
        pt = jnp.exp2(st_s[g] - m).astype(BF16)
        pv = jnp.dot(vt_ref[g * V_ROWS:(g + 1) * V_ROWS, ks], pt, preferred_element_type=F32)
        acc_s[g] = alpha * acc_s[g] + pv

    acc_s[...] = jnp.zeros_like(acc_s)
    key = lax.broadcasted_iota(jnp.int32, (tile, tile), 0)
    qry = lax.broadcasted_iota(jnp.int32, (tile, tile), 1)
    carry = []
    for g in range(group):
        st = jnp.where(key <= qry, logits(g, qi), NEG_BIG)
        st_s[g] = st
        carry.append((jnp.max(st, axis=0, keepdims=True), jnp.ones((1, tile), F32)))

    def step(i, carry):
        prev = jnp.where(i == 0, qi, i - 1)
        out = []
        for g in range(group):
            m, alpha = carry[g]
            st = logits(g, i)
            accumulate(g, prev, m, alpha)
            m_new = jnp.maximum(m, jnp.max(st, axis=0, keepdims=True))
            st_s[g] = st
            out.append((m_new, jnp.exp2(m - m_new)))
        return tuple(out)

    carry = lax.fori_loop(0, qi, step, tuple(carry))
    last = jnp.where(qi == 0, qi, qi - 1)
    for g, ln in enumerate(lanes):
        accumulate(g, last, *carry[g])
        acc = acc_s[g]
        o_ref[:, ln] = (acc[:HEAD_DIM] / acc[HEAD_DIM:HEAD_DIM + 1]).T.astype(o_ref.dtype)


def _fox(proj, qa, ka, vt, n_heads, col_q, col_k, tile, group):
    bsz, seq, _ = proj.shape
    width = group * HEAD_DIM
    assert n_heads % group == 0 and col_q % group == 0 and col_k % group == 0
    qblk = lambda c0: pl.BlockSpec((None, tile, width), lambda b, h, i: (b, i, c0 // group + h))
    kblk = lambda c0: pl.BlockSpec((None, seq, width), lambda b, h, i: (b, 0, c0 // group + h),
                                   pipeline_mode=pl.Buffered(1))
    return pl.pallas_call(
        functools.partial(_fox_body, tile=tile, group=group),
        grid=(bsz, n_heads // group, seq // tile),
        in_specs=[qblk(col_q), qblk(0), kblk(col_k),
                  pl.BlockSpec((None, seq, LANES), lambda b, h, i: (b, 0, 0), pipeline_mode=pl.Buffered(1)),
                  pl.BlockSpec((None, group * V_ROWS, seq), lambda b, h, i: (b, h, 0),
                               pipeline_mode=pl.Buffered(1))],
        out_specs=pl.BlockSpec((None, tile, width), lambda b, h, i: (b, i, h)),
        out_shape=jax.ShapeDtypeStruct((bsz, seq, n_heads * HEAD_DIM), BF16),
        scratch_shapes=[pltpu.VMEM((group, V_ROWS, tile), F32), pltpu.VMEM((group, tile, tile), F32)],
        compiler_params=_params(("parallel", "parallel", "arbitrary")),
        name="fox",
    )(proj, qa, proj, ka, vt)


def _mem_kv_body(mem_ref, g_ref, w_ref, out_ref):
    mn = _rms(mem_ref[...], g_ref[...]).astype(BF16)
    out_ref[...] = jnp.dot(mn, w_ref[...], preferred_element_type=F32).astype(out_ref.dtype)


def _mem_kv(mem, g, w):
    bsz, m, d = mem.shape
    cols = w.shape[1]
    return pl.pallas_call(
        _mem_kv_body,
        grid=(bsz,),
        in_specs=[pl.BlockSpec((None, m, d), lambda b: (b, 0, 0)),
                  pl.BlockSpec((1, d), lambda b: (0, 0)),
                  pl.BlockSpec((d, cols), lambda b: (0, 0))],
        out_specs=pl.BlockSpec((None, m, cols), lambda b: (b, 0, 0)),
        out_shape=jax.ShapeDtypeStruct((bsz, m, cols), BF16),
        compiler_params=_params(("parallel",)),
        name="mem_kv",
    )(mem, g, w)


def _mem_attn_body(u_ref, wq_ref, kv_ref, o_ref, *, n_heads):
    width = n_heads * HEAD_DIM
    scale = HEAD_DIM ** -0.5
    q = (jnp.dot(u_ref[...], wq_ref[...], preferred_element_type=F32) * scale).astype(BF16)
    heads = [slice(h * HEAD_DIM, (h + 1) * HEAD_DIM) for h in range(n_heads)]
    scores = [_dot_nt(q[:, hd], kv_ref[:, hd]) for hd in heads]
    for hd, s in zip(heads, scores):
        p = jnp.exp(s - jnp.max(s, axis=1, keepdims=True))
        p = p / jnp.sum(p, axis=1, keepdims=True)
        o = jnp.dot(p.astype(BF16), kv_ref[:, width + hd.start:width + hd.stop],
                    preferred_element_type=F32)
        o_ref[:, hd] = o.astype(o_ref.dtype)


def _mem_attn(u, w_mq, mem_kv, n_heads, tq):
    bsz, seq, d = u.shape
    width = n_heads * HEAD_DIM
    m = mem_kv.shape[1]
    return pl.pallas_call(
        functools.partial(_mem_attn_body, n_heads=n_heads),
        grid=(bsz, seq // tq),
        in_specs=[pl.BlockSpec((None, tq, d), lambda b, i: (b, i, 0)),
                  pl.BlockSpec((d, width), lambda b, i: (0, 0)),
                  pl.BlockSpec((None, m, 2 * width), lambda b, i: (b, 0, 0))],
        out_specs=pl.BlockSpec((None, tq, width), lambda b, i: (b, i, 0)),
        out_shape=jax.ShapeDtypeStruct((bsz, seq, width), BF16),
        compiler_params=_params(("parallel", "parallel")),
        name="mem_attn",
    )(u, w_mq, mem_kv)


def _merge_body(oh_ref, of_ref, om_ref, g0_ref, g1_ref, g2_ref, x_ref, post_ref,
                wh_ref, wf_ref, wm_ref, wo_ref, out_ref):
    merged = g0_ref[...].astype(F32) * jnp.dot(oh_ref[...], wh_ref[...], preferred_element_type=F32)
    merged += g1_ref[...].astype(F32) * jnp.dot(of_ref[...], wf_ref[...], preferred_element_type=F32)
    merged += g2_ref[...].astype(F32) * jnp.dot(om_ref[...], wm_ref[...], preferred_element_type=F32)
    y = jnp.dot(merged.astype(BF16), wo_ref[...], preferred_element_type=F32)
    out_ref[...] = x_ref[...] + _rms(y, post_ref[...])


def _merge(o_h, o_f, o_m, gates, x1, post, wh, wf, wm, wo, tm):
    n, d = x1.shape
    row = lambda width: pl.BlockSpec((tm, width), lambda i: (i, 0))
    gate = lambda k: pl.BlockSpec((tm, d), lambda i, k=k: (i, k))
    full = lambda w: pl.BlockSpec(w.shape, lambda i: (0, 0), pipeline_mode=pl.Buffered(1))
    return pl.pallas_call(
        _merge_body,
        grid=(n // tm,),
        in_specs=[row(o_h.shape[1]), row(o_f.shape[1]), row(o_m.shape[1]),
                  gate(0), gate(1), gate(2), row(d),
                  pl.BlockSpec((1, d), lambda i: (0, 0)),
                  full(wh), full(wf), full(wm), full(wo)],
        out_specs=row(d),
        out_shape=jax.ShapeDtypeStruct((n, d), F32),
        compiler_params=_params(("parallel",)),
        name="merge",
    )(o_h, o_f, o_m, gates, gates, gates, x1, post, wh, wf, wm, wo)


def _pad_cols(w, to):
    return jnp.pad(w, ((0, 0), (0, to - w.shape[1])))


def _ffn_weights(wg, wu, wd):
    return wg.astype(BF16), wu.astype(BF16), wd.astype(BF16)


def kernel(x, mem, ffn1_pre, ffn1_post, ffn1_wg, ffn1_wu, ffn1_wd, mix_pre, mix_post, mem_norm,
           w_in, hgrn_lb, hgrn_gnorm, fox_fb, w_mem_kv, w_hgrn_out, w_fox_out, w_mem_out, w_gate,
           w_o, ffn2_pre, ffn2_post, ffn2_wg, ffn2_wu, ffn2_wd):
    bsz, seq, d = x.shape
    n = bsz * seq
    depth = ffn1_pre.shape[0]
    wh, wf, wm = w_hgrn_out.shape[1], w_fox_out.shape[1], w_mem_out.shape[1]
    h_hgrn, h_fox, h_mem = wh // HEAD_DIM, wf // HEAD_DIM, wm // HEAD_DIM
    c_fox = 4 * wh
    c_ff = c_fox + 3 * wf
    c_mq = c_ff + h_fox
    n_gate = w_gate.shape[2]
    assert h_fox <= LANES and n_gate == 3 * d
    blk = lambda c: c // HEAD_DIM
    t = _tiles(n, seq, c_ff, n_gate, h_fox, (blk(c_fox), blk(c_fox + wf)))
    scale = np.ones((1, c_ff), np.float32)
    scale[:, c_fox:c_fox + wf] = HEAD_DIM ** -0.5 * LOG2E
    scale = jnp.asarray(scale)
    no_scale = jnp.ones((1, n_gate), F32)

    x2 = x.reshape(n, d)
    vec = lambda a: a.reshape(1, -1)
    for l in range(depth):
        wg1, wu1, wd1 = _ffn_weights(ffn1_wg[l], ffn1_wu[l], ffn1_wd[l])
        x1, u = _ffn(x2, vec(ffn1_pre[l]), vec(ffn1_post[l]), vec(mix_pre[l]), wg1, wu1, wd1,
                     t["ffn_tm"], t["ffn_tf"])
        u3 = u.reshape(bsz, seq, d)

        proj = _proj(u, w_in, l, c_ff, scale, False, t["proj_tm"], t["proj_tn"], "in_proj")
        gates = _proj(u, w_gate, l, n_gate, no_scale, True, t["proj_tm"], t["gate_tn"], "gate_proj")
        proj3 = proj.reshape(bsz, seq, -1)

        o_h = _hgrn(proj3, hgrn_lb, l, vec(hgrn_gnorm[l]), h_hgrn, 0, t["hgrn_chunk"], t["hgrn_tt"])

        w_tail = w_in[l][:, c_ff:]
        w_ff = _pad_cols(w_tail[:, :h_fox], LANES).astype(BF16)
        qa, ka, vt = _fox_prep(u3, w_ff, _pad_cols(vec(fox_fb[l]), LANES), proj3,
                               blk(c_fox + 2 * wf), h_fox, t["prep_tt"])
        o_f = _fox(proj3, qa, ka, vt, h_fox, blk(c_fox), blk(c_fox + wf), t["fox_t"], t["fox_group"])

        mkv = _mem_kv(mem, vec(mem_norm[l]), w_mem_kv[l].astype(BF16))
        o_m = _mem_attn(u3, w_tail[:, h_fox:].astype(BF16), mkv, h_mem, t["mem_tq"])

        x2 = _merge(o_h.reshape(n, wh), o_f.reshape(n, wf), o_m.reshape(n, wm), gates,
                    x1, vec(mix_post[l]), w_hgrn_out[l].astype(BF16), w_fox_out[l].astype(BF16),
                    w_mem_out[l].astype(BF16), w_o[l].astype(BF16), t["merge_tm"])

        wg2, wu2, wd2 = _ffn_weights(ffn2_wg[l], ffn2_wu[l], ffn2_wd[l])
        x2 = _ffn(x2, vec(ffn2_pre[l]), vec(ffn2_post[l]), None, wg2, wu2, wd2,
                  t["ffn_tm"], t["ffn_tf"])
    return x2.reshape(bsz, seq, d)
```

```python
import functools
import math

import numpy as np
import jax
import jax.numpy as jnp
from jax import lax
from jax.experimental import pallas as pl
from jax.experimental.pallas import tpu as pltpu

F32 = jnp.float32
BF16 = jnp.bfloat16
EPS = 1e-6
HEAD_DIM = 128
LANES = 128
LOG2E = math.log2(math.e)
VMEM_LIMIT_BYTES = 56 * 1024 * 1024
NEG_BIG = -1e30


def _tiles(n_tokens, seq, proj_cols, gate_cols, fox_heads, fox_head_cols):
    widest = lambda cols: next(w for w in (1024, 768, 512, 256, 128) if cols % w == 0)
    fox_group = next(g for g in (6, 3, 2, 1)
                     if fox_heads % g == 0 and all(c % g == 0 for c in fox_head_cols))
    return dict(
        ffn_tm=min(512, n_tokens), ffn_tf=1024,
        proj_tm=min(1024, n_tokens), proj_tn=widest(proj_cols), gate_tn=widest(gate_cols),
        hgrn_chunk=min(128, seq), hgrn_tt=min(2048, seq),
        prep_tt=min(256, seq),
        fox_t=min(512, seq), fox_group=fox_group,
        mem_tq=min(512, seq),
        merge_tm=min(512, n_tokens),
    )


def _params(sem):
    return pltpu.CompilerParams(dimension_semantics=sem, vmem_limit_bytes=VMEM_LIMIT_BYTES)


def _rms(x, g):
    ms = jnp.mean(x * x, axis=-1, keepdims=True)
    return x * lax.rsqrt(ms + EPS) * g


def _sigmoid(x):
    return 0.5 * jnp.tanh(0.5 * x) + 0.5


def _silu(x):
    return x * _sigmoid(x)


def _dot_nt(a, b):
    return lax.dot_general(a, b, (((1,), (1,)), ((), ())), preferred_element_type=F32)


def _dot_tn(a, b):
    return lax.dot_general(a, b, (((0,), (0,)), ((), ())), preferred_element_type=F32)


def _ffn_body(*refs, emit_next, ragged_width, n_tiles):
    if emit_next:
        x_ref, pre_ref, post_ref, nxt_ref, wg_ref, wu_ref, wd_ref, out_ref, u_ref, xn_s, acc_s = refs
    else:
        x_ref, pre_ref, post_ref, wg_ref, wu_ref, wd_ref, out_ref, xn_s, acc_s = refs
    j = pl.program_id(1)
    last = pl.num_programs(1) - 1
    full_width = wg_ref.shape[1]

    def hidden_tile(width):
        xn = xn_s[...]
        g = jnp.dot(xn, wg_ref[:, :width], preferred_element_type=F32)
        u = jnp.dot(xn, wu_ref[:, :width], preferred_element_type=F32)
        h = (_silu(g) * u).astype(BF16)
        return jnp.dot(h, wd_ref[:width, :], preferred_element_type=F32)

    @pl.when(j == 0)
    def _():
        xn_s[...] = _rms(x_ref[...], pre_ref[...]).astype(BF16)
        acc_s[...] = hidden_tile(ragged_width)

    @pl.when(jnp.logical_and(j > 0, j < last))
    def _():
        acc_s[...] += hidden_tile(full_width)

    @pl.when(j == last)
    def _():
        if n_tiles > 1:
            acc_s[...] += hidden_tile(full_width)
        x1 = x_ref[...] + 0.5 * _rms(acc_s[...], post_ref[...])
        out_ref[...] = x1
        if emit_next:
            u_ref[...] = _rms(x1, nxt_ref[...]).astype(BF16)


def _ffn(x, pre, post, nxt, wg, wu, wd, tm, tf):
    n, d = x.shape
    f = wg.shape[1]
    n_f = pl.cdiv(f, tf)
    emit_next = nxt is not None
    row = lambda i, j: (i, 0)
    vec = pl.BlockSpec((1, d), lambda i, j: (0, 0))
    in_specs = [pl.BlockSpec((tm, d), row), vec, vec]
    args = [x, pre, post]
    if emit_next:
        in_specs.append(vec)
        args.append(nxt)
    tile = lambda j: (j + n_f - 1) % n_f
    in_specs += [pl.BlockSpec((d, tf), lambda i, j: (0, tile(j))),
                 pl.BlockSpec((d, tf), lambda i, j: (0, tile(j))),
                 pl.BlockSpec((tf, d), lambda i, j: (tile(j), 0))]
    args += [wg, wu, wd]
    out_shape = [jax.ShapeDtypeStruct((n, d), F32)]
    out_specs = [pl.BlockSpec((tm, d), row)]
    if emit_next:
        out_shape.append(jax.ShapeDtypeStruct((n, d), BF16))
        out_specs.append(pl.BlockSpec((tm, d), row))
    res = pl.pallas_call(
        functools.partial(_ffn_body, emit_next=emit_next, ragged_width=f - (n_f - 1) * tf,
                          n_tiles=n_f),
        grid=(n // tm, n_f),
        in_specs=in_specs, out_specs=out_specs, out_shape=out_shape,
        scratch_shapes=[pltpu.VMEM((tm, d), BF16), pltpu.VMEM((tm, d), F32)],
        compiler_params=_params(("parallel", "arbitrary")),
        name="ffn_next" if emit_next else "ffn",
    )(*args)
    return res if emit_next else res[0]


def _proj_body(u_ref, w_ref, s_ref, out_ref, wb_s, *, gate):
    @pl.when(pl.program_id(1) == 0)
    def _():
        wb_s[...] = w_ref[...].astype(BF16)

    acc = jnp.dot(u_ref[...], wb_s[...], preferred_element_type=F32)
    if gate:
        out_ref[...] = _sigmoid(acc).astype(out_ref.dtype)
    else:
        out_ref[...] = (acc * s_ref[...]).astype(out_ref.dtype)


def _proj(u, w, layer, cols, scale, gate, tm, tn, name):
    n, d = u.shape
    assert cols % tn == 0
    return pl.pallas_call(
        functools.partial(_proj_body, gate=gate),
        grid=(cols // tn, n // tm),
        in_specs=[pl.BlockSpec((tm, d), lambda j, i: (i, 0)),
                  pl.BlockSpec((None, d, tn), lambda j, i: (layer, 0, j)),
                  pl.BlockSpec((1, tn), lambda j, i: (0, j))],
        out_specs=pl.BlockSpec((tm, tn), lambda j, i: (i, j)),
        out_shape=jax.ShapeDtypeStruct((n, cols), BF16),
        scratch_shapes=[pltpu.VMEM((d, tn), BF16)],
        compiler_params=_params(("parallel", "arbitrary")),
        name=name,
    )(u, w, scale)


def _hgrn_levels(chunk):
    return [chunk >> (i + 1) for i in range(int(math.log2(chunk)))]


def _hgrn_constants(chunk):
    t = np.arange(chunk)[:, None]
    r = np.arange(chunk)[None, :]
    blocks = [(r <= t), (r > t)]
    masks = [(r == t)]
    for h in _hgrn_levels(chunk):
        ref = (t // (2 * h)) * (2 * h) + h - 1
        upper = (t & h) != 0
        blocks.append(np.where(upper, (r > ref) & (r <= t), (r > t) & (r <= ref)))
        masks.append(((t ^ r) >= h) & ((t ^ r) < 2 * h) & upper)
    a = np.concatenate(blocks, axis=0).astype(np.float32)
    m = np.stack(masks, axis=0).astype(np.float32)
    return jnp.asarray(np.concatenate([a, a], axis=1), BF16), jnp.asarray(m, F32)


def _hgrn_body(lb_ref, gn_ref, a_ref, mask_ref, hq_ref, hf_ref, hi_ref, hog_ref, o_ref, st_s, ex_s,
               *, chunk, n_chunks, layer):
    @pl.when(pl.program_id(2) == 0)
    def _():
        st_s[...] = jnp.zeros_like(st_s)

    slots = [lb_ref[r:r + 1, :] for r in range(lb_ref.shape[0])]
    top = functools.reduce(jnp.maximum, slots)
    es = [jnp.exp(s - top) for s in slots]
    lb = sum(es[1:layer + 1], es[0]) / sum(es[1:], es[0])
    gn = gn_ref[...]
    levels = _hgrn_levels(chunk)
    row = lax.broadcasted_iota(jnp.int32, (chunk, HEAD_DIM), 0)
    rows = [slice(c * chunk, (c + 1) * chunk) for c in range(n_chunks)]

    g = lb + (1.0 - lb) * _sigmoid(hf_ref[...].astype(F32))
    lg = jnp.log(g)
    lg_hi = lg.astype(BF16)
    lg_lo = (lg - lg_hi.astype(F32)).astype(BF16)
    pieces = jnp.concatenate([jnp.concatenate([p[r] for r in rows], axis=1) for p in (lg_hi, lg_lo)],
                             axis=0)
    ex_s[...] = jnp.exp(jnp.dot(a_ref[...], pieces, preferred_element_type=F32))

    def intra(c):
        r, ln = rows[c], slice(c * HEAD_DIM, (c + 1) * HEAD_DIM)
        kk = 1.0 - g[r]
        q = _silu(hq_ref[r, :].astype(F32))
        scores = _dot_nt(q.astype(BF16), kk.astype(BF16)) * mask_ref[0]
        for li, h in enumerate(levels):
            f = ex_s[(2 + li) * chunk:(3 + li) * chunk, ln]
            y = (jnp.where((row & h) != 0, q, kk) * f).astype(BF16)
            scores = scores + _dot_nt(y, y) * mask_ref[1 + li]
        o_intra = jnp.dot(scores.astype(BF16), hi_ref[r, :], preferred_element_type=F32)
        q_in = (q * ex_s[0:chunk, ln]).astype(BF16)
        k_end = (kk * ex_s[chunk:2 * chunk, ln]).astype(BF16)
        kv = _dot_tn(hi_ref[r, :], k_end)
        return o_intra, q_in, kv

    def finish(c, o_intra, q_in, kv):
        r, ln = rows[c], slice(c * HEAD_DIM, (c + 1) * HEAD_DIM)
        st = st_s[...]
        o = o_intra + _dot_nt(q_in, st.astype(BF16))
        st_s[...] = st * ex_s[chunk - 1:chunk, ln] + kv
        o = _rms(o, gn) * _silu(hog_ref[r, :].astype(F32))
        o_ref[r, :] = o.astype(o_ref.dtype)

    ahead = intra(0)
    for c in range(n_chunks):
        cur, ahead = ahead, (intra(c + 1) if c + 1 < n_chunks else None)
        finish(c, *cur)


def _hgrn(proj, hgrn_lb, layer, gnorm, n_heads, col0, chunk, tt):
    bsz, seq, _ = proj.shape
    a_mat, masks = _hgrn_constants(chunk)
    n_slots = hgrn_lb.shape[0]

    def col(k):
        return pl.BlockSpec((None, tt, HEAD_DIM), lambda b, h, t, k=k: (b, t, col0 + k * n_heads + h))

    return pl.pallas_call(
        functools.partial(_hgrn_body, chunk=chunk, n_chunks=tt // chunk, layer=layer),
        grid=(bsz, n_heads, seq // tt),
        in_specs=[pl.BlockSpec((n_slots, HEAD_DIM), lambda b, h, t: (0, h)),
                  pl.BlockSpec((1, HEAD_DIM), lambda b, h, t: (0, h)),
                  pl.BlockSpec(a_mat.shape, lambda b, h, t: (0, 0)),
                  pl.BlockSpec(masks.shape, lambda b, h, t: (0, 0, 0)),
                  col(0), col(1), col(2), col(3)],
        out_specs=pl.BlockSpec((None, tt, HEAD_DIM), lambda b, h, t: (b, t, h)),
        out_shape=jax.ShapeDtypeStruct((bsz, seq, n_heads * HEAD_DIM), BF16),
        scratch_shapes=[pltpu.VMEM((HEAD_DIM, HEAD_DIM), F32),
                        pltpu.VMEM((a_mat.shape[0], (tt // chunk) * HEAD_DIM), F32)],
        compiler_params=_params(("parallel", "parallel", "arbitrary")),
        name="hgrn",
    )(hgrn_lb, gnorm, a_mat, masks, proj, proj, proj, proj)


def _split3(x):
    hi = x.astype(BF16).astype(F32)
    r = x - hi
    mid = r.astype(BF16).astype(F32)
    lo = (r - mid).astype(BF16).astype(F32)
    return hi, mid, lo


V_ROWS = HEAD_DIM + 16


def _fox_bias_placement(n_heads):
    width = n_heads * LANES
    assert 6 * n_heads <= LANES
    place = np.zeros((3 * LANES, width + LANES), np.float32)
    const = np.zeros((1, width + LANES), np.float32)
    for h in range(n_heads):
        for p in range(3):
            place[p * LANES + h, h * LANES + 6 * h + p] = 1.0
            place[p * LANES + h, width + 6 * h + 3 + p] = -1.0
        const[0, h * LANES + 6 * h + 3:h * LANES + 6 * h + 6] = 1.0
        const[0, width + 6 * h:width + 6 * h + 3] = 1.0
    return jnp.asarray(place, BF16), jnp.asarray(const, F32)


def _fox_prep_body(u_ref, wff_ref, fb_ref, tri_ref, place_ref, const_ref, v_ref, qa_ref, ka_ref, vt_ref,
                   carry_s, *, n_heads):
    @pl.when(pl.program_id(1) == 0)
    def _():
        carry_s[...] = jnp.zeros_like(carry_s)

    tt = u_ref.shape[0]
    width = n_heads * LANES
    ones = jnp.ones((V_ROWS - HEAD_DIM, tt), BF16)
    for h in range(n_heads):
        vt_ref[h * V_ROWS:h * V_ROWS + HEAD_DIM, :] = (
            v_ref[:, h * LANES:(h + 1) * LANES].astype(F32).T.astype(BF16))
        vt_ref[h * V_ROWS + HEAD_DIM:(h + 1) * V_ROWS, :] = ones
    z = jnp.dot(u_ref[...], wff_ref[...], preferred_element_type=F32) + fb_ref[...]
    log_f = jnp.minimum(z, 0.0) - jnp.log(1.0 + jnp.exp(-jnp.abs(z)))
    tri = tri_ref[...]
    c = carry_s[...]
    for piece in _split3(log_f):
        c = c + jnp.dot(tri, piece.astype(BF16), preferred_element_type=F32)
    carry_s[...] = c[tt - 1:tt, :]
    pieces = jnp.concatenate(_split3(c * LOG2E), axis=1).astype(BF16)
    bias = jnp.dot(pieces, place_ref[...], preferred_element_type=F32) + const_ref[...]
    qa_ref[...] = bias[:, :width].astype(BF16)
    ka_ref[...] = bias[:, width:].astype(BF16)


def _fox_prep(u, w_ff, fox_b, proj, col_v, n_heads, tt):
    bsz, seq, d = u.shape
    width = n_heads * LANES
    assert col_v % n_heads == 0
    tri = jnp.asarray(np.tril(np.ones((tt, tt), np.float32)), BF16)
    place, const = _fox_bias_placement(n_heads)
    out = jax.ShapeDtypeStruct((bsz, seq, width), BF16)
    blk = pl.BlockSpec((None, tt, width), lambda b, t: (b, t, 0))
    whole = lambda a: pl.BlockSpec(a.shape, lambda b, t: (0, 0))
    return pl.pallas_call(
        functools.partial(_fox_prep_body, n_heads=n_heads),
        grid=(bsz, seq // tt),
        in_specs=[pl.BlockSpec((None, tt, d), lambda b, t: (b, t, 0)),
                  whole(w_ff), whole(fox_b), whole(tri), whole(place), whole(const),
                  pl.BlockSpec((None, tt, width), lambda b, t: (b, t, col_v // n_heads))],
        out_specs=[blk, pl.BlockSpec((None, tt, LANES), lambda b, t: (b, t, 0)),
                   pl.BlockSpec((None, None, n_heads * V_ROWS, tt), lambda b, t: (b, t, 0, 0))],
        out_shape=[out, jax.ShapeDtypeStruct((bsz, seq, LANES), BF16),
                   jax.ShapeDtypeStruct((bsz, seq // tt, n_heads * V_ROWS, tt), BF16)],
        scratch_shapes=[pltpu.VMEM((1, LANES), F32)],
        compiler_params=_params(("parallel", "arbitrary")),
        name="fox_prep",
    )(u, w_ff, fox_b, tri, place, const, proj)


def _fox_body(q_ref, qa_ref, k_ref, ka_ref, vt_ref, o_ref, acc_s, st_s, *, tile, group):
    qi = pl.program_id(2)
    lanes = [slice(g * HEAD_DIM, (g + 1) * HEAD_DIM) for g in range(group)]
    qcs = [jnp.concatenate([q_ref[:, ln], qa_ref[:, ln]], axis=1) for ln in lanes]

    def logits(g, j):
        ks = pl.ds(pl.multiple_of(j * tile, tile), tile)
        kc = jnp.concatenate([k_ref[ks, lanes[g]], ka_ref[ks, :]], axis=1)
        return _dot_nt(kc, qcs[g])

    def accumulate(g, j, m, alpha):
        slabs = tile // vt_ref.shape[2]
        vt = jnp.concatenate([vt_ref[j * slabs + k, g * V_ROWS:(g + 1) * V_ROWS, :]
                              for k in range(slabs)], axis=1)
        pt = jnp.exp2(st_s[g] - m).astype(BF16)
        pv = jnp.dot(vt, pt, preferred_element_type=F32)
        acc_s[g] = alpha * acc_s[g] + pv

    acc_s[...] = jnp.zeros_like(acc_s)
    key = lax.broadcasted_iota(jnp.int32, (tile, tile), 0)
    qry = lax.broadcasted_iota(jnp.int32, (tile, tile), 1)
    carry = []
    for g in range(group):
        st = jnp.where(key <= qry, logits(g, qi), NEG_BIG)
        st_s[g] = st
        carry.append((jnp.max(st, axis=0, keepdims=True), jnp.ones((1, tile), F32)))

    def step(i, carry):
        prev = jnp.where(i == 0, qi, i - 1)
        out = []
        for g in range(group):
            m, alpha = carry[g]
            st = logits(g, i)
            accumulate(g, prev, m, alpha)
            m_new = jnp.maximum(m, jnp.max(st, axis=0, keepdims=True))
            st_s[g] = st
            out.append((m_new, jnp.exp2(m - m_new)))
        return tuple(out)

    carry = lax.fori_loop(0, qi, step, tuple(carry))
    last = jnp.where(qi == 0, qi, qi - 1)
    for g, ln in enumerate(lanes):
        accumulate(g, last, *carry[g])
        acc = acc_s[g]
        o_ref[:, ln] = (acc[:HEAD_DIM] / acc[HEAD_DIM:HEAD_DIM + 1]).T.astype(o_ref.dtype)


def _fox(proj, qa, ka, vt, n_heads, col_q, col_k, tile, group):
    bsz, seq, _ = proj.shape
    width = group * HEAD_DIM
    assert n_heads % group == 0 and col_q % group == 0 and col_k % group == 0
    assert tile % vt.shape[3] == 0
    qblk = lambda c0: pl.BlockSpec((None, tile, width), lambda b, h, i: (b, i, c0 // group + h))
    kblk = lambda c0: pl.BlockSpec((None, seq, width), lambda b, h, i: (b, 0, c0 // group + h),
                                   pipeline_mode=pl.Buffered(1))
    return pl.pallas_call(
        functools.partial(_fox_body, tile=tile, group=group),
        grid=(bsz, n_heads // group, seq // tile),
        in_specs=[qblk(col_q), qblk(0), kblk(col_k),
                  pl.BlockSpec((None, seq, LANES), lambda b, h, i: (b, 0, 0), pipeline_mode=pl.Buffered(1)),
                  pl.BlockSpec((None, vt.shape[1], group * V_ROWS, vt.shape[3]),
                               lambda b, h, i: (b, 0, h, 0), pipeline_mode=pl.Buffered(1))],
        out_specs=pl.BlockSpec((None, tile, width), lambda b, h, i: (b, i, h)),
        out_shape=jax.ShapeDtypeStruct((bsz, seq, n_heads * HEAD_DIM), BF16),
        scratch_shapes=[pltpu.VMEM((group, V_ROWS, tile), F32), pltpu.VMEM((group, tile, tile), F32)],
        compiler_params=_params(("parallel", "parallel", "arbitrary")),
        name="fox",
    )(proj, qa, proj, ka, vt)


def _mem_kv_body(mem_ref, g_ref, w_ref, out_ref):
    mn = _rms(mem_ref[...], g_ref[...]).astype(BF16)
    out_ref[...] = jnp.dot(mn, w_ref[...], preferred_element_type=F32).astype(out_ref.dtype)


def _mem_kv(mem, g, w):
    bsz, m, d = mem.shape
    cols = w.shape[1]
    return pl.pallas_call(
        _mem_kv_body,
        grid=(bsz,),
        in_specs=[pl.BlockSpec((None, m, d), lambda b: (b, 0, 0)),
                  pl.BlockSpec((1, d), lambda b: (0, 0)),
                  pl.BlockSpec((d, cols), lambda b: (0, 0))],
        out_specs=pl.BlockSpec((None, m, cols), lambda b: (b, 0, 0)),
        out_shape=jax.ShapeDtypeStruct((bsz, m, cols), BF16),
        compiler_params=_params(("parallel",)),
        name="mem_kv",
    )(mem, g, w)


def _mem_attn_body(u_ref, wq_ref, kv_ref, o_ref, *, n_heads):
    width = n_heads * HEAD_DIM
    scale = HEAD_DIM ** -0.5
    q = (jnp.dot(u_ref[...], wq_ref[...], preferred_element_type=F32) * scale).astype(BF16)
    heads = [slice(h * HEAD_DIM, (h + 1) * HEAD_DIM) for h in range(n_heads)]
    scores = [_dot_nt(q[:, hd], kv_ref[:, hd]) for hd in heads]
    for hd, s in zip(heads, scores):
        p = jnp.exp(s - jnp.max(s, axis=1, keepdims=True))
        p = p / jnp.sum(p, axis=1, keepdims=True)
        o = jnp.dot(p.astype(BF16), kv_ref[:, width + hd.start:width + hd.stop],
                    preferred_element_type=F32)
        o_ref[:, hd] = o.astype(o_ref.dtype)


def _mem_attn(u, w_mq, mem_kv, n_heads, tq):
    bsz, seq, d = u.shape
    width = n_heads * HEAD_DIM
    m = mem_kv.shape[1]
    return pl.pallas_call(
        functools.partial(_mem_attn_body, n_heads=n_heads),
        grid=(bsz, seq // tq),
        in_specs=[pl.BlockSpec((None, tq, d), lambda b, i: (b, i, 0)),
                  pl.BlockSpec((d, width), lambda b, i: (0, 0)),
                  pl.BlockSpec((None, m, 2 * width), lambda b, i: (b, 0, 0))],
        out_specs=pl.BlockSpec((None, tq, width), lambda b, i: (b, i, 0)),
        out_shape=jax.ShapeDtypeStruct((bsz, seq, width), BF16),
        compiler_params=_params(("parallel", "parallel")),
        name="mem_attn",
    )(u, w_mq, mem_kv)


def _merge_body(oh_ref, of_ref, om_ref, g0_ref, g1_ref, g2_ref, x_ref, post_ref,
                wh_ref, wf_ref, wm_ref, wo_ref, out_ref):
    merged = g0_ref[...].astype(F32) * jnp.dot(oh_ref[...], wh_ref[...], preferred_element_type=F32)
    merged += g1_ref[...].astype(F32) * jnp.dot(of_ref[...], wf_ref[...], preferred_element_type=F32)
    merged += g2_ref[...].astype(F32) * jnp.dot(om_ref[...], wm_ref[...], preferred_element_type=F32)
    y = jnp.dot(merged.astype(BF16), wo_ref[...], preferred_element_type=F32)
    out_ref[...] = x_ref[...] + _rms(y, post_ref[...])


def _merge(o_h, o_f, o_m, gates, x1, post, wh, wf, wm, wo, tm):
    n, d = x1.shape
    row = lambda width: pl.BlockSpec((tm, width), lambda i: (i, 0))
    gate = lambda k: pl.BlockSpec((tm, d), lambda i, k=k: (i, k))
    full = lambda w: pl.BlockSpec(w.shape, lambda i: (0, 0), pipeline_mode=pl.Buffered(1))
    return pl.pallas_call(
        _merge_body,
        grid=(n // tm,),
        in_specs=[row(o_h.shape[1]), row(o_f.shape[1]), row(o_m.shape[1]),
                  gate(0), gate(1), gate(2), row(d),
                  pl.BlockSpec((1, d), lambda i: (0, 0)),
                  full(wh), full(wf), full(wm), full(wo)],
        out_specs=row(d),
        out_shape=jax.ShapeDtypeStruct((n, d), F32),
        compiler_params=_params(("parallel",)),
        name="merge",
    )(o_h, o_f, o_m, gates, gates, gates, x1, post, wh, wf, wm, wo)


def _pad_cols(w, to):
    return jnp.pad(w, ((0, 0), (0, to - w.shape[1])))


def _ffn_weights(wg, wu, wd):
    return wg.astype(BF16), wu.astype(BF16), wd.astype(BF16)


def kernel(x, mem, ffn1_pre, ffn1_post, ffn1_wg, ffn1_wu, ffn1_wd, mix_pre, mix_post, mem_norm,
           w_in, hgrn_lb, hgrn_gnorm, fox_fb, w_mem_kv, w_hgrn_out, w_fox_out, w_mem_out, w_gate,
           w_o, ffn2_pre, ffn2_post, ffn2_wg, ffn2_wu, ffn2_wd):
    bsz, seq, d = x.shape
    n = bsz * seq
    depth = ffn1_pre.shape[0]
    wh, wf, wm = w_hgrn_out.shape[1], w_fox_out.shape[1], w_mem_out.shape[1]
    h_hgrn, h_fox, h_mem = wh // HEAD_DIM, wf // HEAD_DIM, wm // HEAD_DIM
    c_fox = 4 * wh
    c_ff = c_fox + 3 * wf
    c_mq = c_ff + h_fox
    n_gate = w_gate.shape[2]
    assert h_fox <= LANES and n_gate == 3 * d
    blk = lambda c: c // HEAD_DIM
    t = _tiles(n, seq, c_ff, n_gate, h_fox, (blk(c_fox), blk(c_fox + wf)))
    scale = np.ones((1, c_ff), np.float32)
    scale[:, c_fox:c_fox + wf] = HEAD_DIM ** -0.5 * LOG2E
    scale = jnp.asarray(scale)
    no_scale = jnp.ones((1, n_gate), F32)

    x2 = x.reshape(n, d)
    vec = lambda a: a.reshape(1, -1)
    for l in range(depth):
        wg1, wu1, wd1 = _ffn_weights(ffn1_wg[l], ffn1_wu[l], ffn1_wd[l])
        x1, u = _ffn(x2, vec(ffn1_pre[l]), vec(ffn1_post[l]), vec(mix_pre[l]), wg1, wu1, wd1,
                     t["ffn_tm"], t["ffn_tf"])
        u3 = u.reshape(bsz, seq, d)

        proj = _proj(u, w_in, l, c_ff, scale, False, t["proj_tm"], t["proj_tn"], "in_proj")
        gates = _proj(u, w_gate, l, n_gate, no_scale, True, t["proj_tm"], t["gate_tn"], "gate_proj")
        proj3 = proj.reshape(bsz, seq, -1)

        o_h = _hgrn(proj3, hgrn_lb, l, vec(hgrn_gnorm[l]), h_hgrn, 0, t["hgrn_chunk"], t["hgrn_tt"])

        w_tail = w_in[l][:, c_ff:]
        w_ff = _pad_cols(w_tail[:, :h_fox], LANES).astype(BF16)
        qa, ka, vt = _fox_prep(u3, w_ff, _pad_cols(vec(fox_fb[l]), LANES), proj3,
                               blk(c_fox + 2 * wf), h_fox, t["prep_tt"])
        o_f = _fox(proj3, qa, ka, vt, h_fox, blk(c_fox), blk(c_fox + wf), t["fox_t"], t["fox_group"])

        mkv = _mem_kv(mem, vec(mem_norm[l]), w_mem_kv[l].astype(BF16))
        o_m = _mem_attn(u3, w_tail[:, h_fox:].astype(BF16), mkv, h_mem, t["mem_tq"])

        x2 = _merge(o_h.reshape(n, wh), o_f.reshape(n, wf), o_m.reshape(n, wm), gates,
                    x1, vec(mix_post[l]), w_hgrn_out[l].astype(BF16), w_fox_out[l].astype(BF16),
                    w_mem_out[l].astype(BF16), w_o[l].astype(BF16), t["merge_tm"])

        wg2, wu2, wd2 = _ffn_weights(ffn2_wg[l], ffn2_wu[l], ffn2_wd[l])
        x2 = _ffn(x2, vec(ffn2_pre[l]), vec(ffn2_post[l]), None, wg2, wu2, wd2,
                  t["ffn_tm"], t["ffn_tf"])
    return x2.reshape(bsz, seq, d)
```

```python
import functools
import math

import numpy as np
import jax
import jax.numpy as jnp
from jax import lax
from jax.experimental import pallas as pl
from jax.experimental.pallas import tpu as pltpu

F32 = jnp.float32
BF16 = jnp.bfloat16
EPS = 1e-6
HEAD_DIM = 128
LANES = 128
LOG2E = math.log2(math.e)
VMEM_LIMIT_BYTES = 56 * 1024 * 1024
NEG_BIG = -1e30


def _tiles(n_tokens, seq, proj_cols, gate_cols, fox_heads, fox_head_cols):
    widest = lambda cols: next(w for w in (1024, 768, 512, 256, 128) if cols % w == 0)
    fox_group = next(g for g in (6, 3, 2, 1)
                     if fox_heads % g == 0 and all(c % g == 0 for c in fox_head_cols))
    return dict(
        ffn_tm=min(512, n_tokens), ffn_tf=1024,
        proj_tm=min(1024, n_tokens), proj_tn=widest(proj_cols), gate_tn=widest(gate_cols),
        hgrn_chunk=min(128, seq), hgrn_tt=min(2048, seq),
        prep_tt=min(256, seq),
        fox_t=min(512, seq), fox_group=fox_group,
        mem_tq=min(512, seq),
        merge_tm=min(512, n_tokens),
    )


def _params(sem):
    return pltpu.CompilerParams(dimension_semantics=sem, vmem_limit_bytes=VMEM_LIMIT_BYTES)


def _rms(x, g):
    ms = jnp.mean(x * x, axis=-1, keepdims=True)
    return x * lax.rsqrt(ms + EPS) * g


def _sigmoid(x):
    return 0.5 * jnp.tanh(0.5 * x) + 0.5


def _silu(x):
    return x * _sigmoid(x)


def _dot_nt(a, b):
    return lax.dot_general(a, b, (((1,), (1,)), ((), ())), preferred_element_type=F32)


def _dot_tn(a, b):
    return lax.dot_general(a, b, (((0,), (0,)), ((), ())), preferred_element_type=F32)


def _ffn_body(*refs, emit_next, ragged_width, n_tiles):
    if emit_next:
        x_ref, pre_ref, post_ref, nxt_ref, wg_ref, wu_ref, wd_ref, out_ref, u_ref, xn_s, acc_s = refs
    else:
        x_ref, pre_ref, post_ref, wg_ref, wu_ref, wd_ref, out_ref, xn_s, acc_s = refs
    j = pl.program_id(1)
    last = pl.num_programs(1) - 1
    full_width = wg_ref.shape[1]

    def hidden_tile(width):
        xn = xn_s[...]
        g = jnp.dot(xn, wg_ref[:, :width], preferred_element_type=F32)
        u = jnp.dot(xn, wu_ref[:, :width], preferred_element_type=F32)
        h = (_silu(g) * u).astype(BF16)
        return jnp.dot(h, wd_ref[:width, :], preferred_element_type=F32)

    @pl.when(j == 0)
    def _():
        xn_s[...] = _rms(x_ref[...], pre_ref[...]).astype(BF16)
        acc_s[...] = hidden_tile(ragged_width)

    @pl.when(jnp.logical_and(j > 0, j < last))
    def _():
        acc_s[...] += hidden_tile(full_width)

    @pl.when(j == last)
    def _():
        if n_tiles > 1:
            acc_s[...] += hidden_tile(full_width)
        x1 = x_ref[...] + 0.5 * _rms(acc_s[...], post_ref[...])
        out_ref[...] = x1
        if emit_next:
            u_ref[...] = _rms(x1, nxt_ref[...]).astype(BF16)


def _ffn(x, pre, post, nxt, wg, wu, wd, tm, tf):
    n, d = x.shape
    f = wg.shape[1]
    n_f = pl.cdiv(f, tf)
    emit_next = nxt is not None
    row = lambda i, j: (i, 0)
    vec = pl.BlockSpec((1, d), lambda i, j: (0, 0))
    in_specs = [pl.BlockSpec((tm, d), row), vec, vec]
    args = [x, pre, post]
    if emit_next:
        in_specs.append(vec)
        args.append(nxt)
    tile = lambda j: (j + n_f - 1) % n_f
    in_specs += [pl.BlockSpec((d, tf), lambda i, j: (0, tile(j))),
                 pl.BlockSpec((d, tf), lambda i, j: (0, tile(j))),
                 pl.BlockSpec((tf, d), lambda i, j: (tile(j), 0))]
    args += [wg, wu, wd]
    out_shape = [jax.ShapeDtypeStruct((n, d), F32)]
    out_specs = [pl.BlockSpec((tm, d), row)]
    if emit_next:
        out_shape.append(jax.ShapeDtypeStruct((n, d), BF16))
        out_specs.append(pl.BlockSpec((tm, d), row))
    res = pl.pallas_call(
        functools.partial(_ffn_body, emit_next=emit_next, ragged_width=f - (n_f - 1) * tf,
                          n_tiles=n_f),
        grid=(n // tm, n_f),
        in_specs=in_specs, out_specs=out_specs, out_shape=out_shape,
        scratch_shapes=[pltpu.VMEM((tm, d), BF16), pltpu.VMEM((tm, d), F32)],
        compiler_params=_params(("parallel", "arbitrary")),
        name="ffn_next" if emit_next else "ffn",
    )(*args)
    return res if emit_next else res[0]


def _proj_body(u_ref, w_ref, s_ref, out_ref, wb_s, *, gate):
    @pl.when(pl.program_id(1) == 0)
    def _():
        wb_s[...] = w_ref[...].astype(BF16)

    acc = jnp.dot(u_ref[...], wb_s[...], preferred_element_type=F32)
    if gate:
        out_ref[...] = _sigmoid(acc).astype(out_ref.dtype)
    else:
        out_ref[...] = (acc * s_ref[...]).astype(out_ref.dtype)


def _proj(u, w, layer, cols, scale, gate, tm, tn, name):
    n, d = u.shape
    assert cols % tn == 0
    return pl.pallas_call(
        functools.partial(_proj_body, gate=gate),
        grid=(cols // tn, n // tm),
        in_specs=[pl.BlockSpec((tm, d), lambda j, i: (i, 0)),
                  pl.BlockSpec((None, d, tn), lambda j, i: (layer, 0, j)),
                  pl.BlockSpec((1, tn), lambda j, i: (0, j))],
        out_specs=pl.BlockSpec((tm, tn), lambda j, i: (i, j)),
        out_shape=jax.ShapeDtypeStruct((n, cols), BF16),
        scratch_shapes=[pltpu.VMEM((d, tn), BF16)],
        compiler_params=_params(("parallel", "arbitrary")),
        name=name,
    )(u, w, scale)


def _hgrn_levels(chunk):
    return [chunk >> (i + 1) for i in range(int(math.log2(chunk)))]


def _hgrn_constants(chunk):
    t = np.arange(chunk)[:, None]
    r = np.arange(chunk)[None, :]
    blocks = [(r <= t), (r > t)]
    masks = [(r == t)]
    for h in _hgrn_levels(chunk):
        ref = (t // (2 * h)) * (2 * h) + h - 1
        upper = (t & h) != 0
        blocks.append(np.where(upper, (r > ref) & (r <= t), (r > t) & (r <= ref)))
        masks.append(((t ^ r) >= h) & ((t ^ r) < 2 * h) & upper)
    a = np.concatenate(blocks, axis=0).astype(np.float32)
    m = np.stack(masks, axis=0).astype(np.float32)
    return jnp.asarray(np.concatenate([a, a], axis=1), BF16), jnp.asarray(m, F32)


def _hgrn_body(lb_ref, gn_ref, a_ref, mask_ref, hq_ref, hf_ref, hi_ref, hog_ref, o_ref, st_s, ex_s,
               *, chunk, n_chunks, layer):
    @pl.when(pl.program_id(2) == 0)
    def _():
        st_s[...] = jnp.zeros_like(st_s)

    slots = [lb_ref[r:r + 1, :] for r in range(lb_ref.shape[0])]
    top = functools.reduce(jnp.maximum, slots)
    es = [jnp.exp(s - top) for s in slots]
    lb = sum(es[1:layer + 1], es[0]) / sum(es[1:], es[0])
    gn = gn_ref[...]
    levels = _hgrn_levels(chunk)
    row = lax.broadcasted_iota(jnp.int32, (chunk, HEAD_DIM), 0)
    rows = [slice(c * chunk, (c + 1) * chunk) for c in range(n_chunks)]

    g = lb + (1.0 - lb) * _sigmoid(hf_ref[...].astype(F32))
    lg = jnp.log(g)
    lg_hi = lg.astype(BF16)
    lg_lo = (lg - lg_hi.astype(F32)).astype(BF16)
    pieces = jnp.concatenate([jnp.concatenate([p[r] for r in rows], axis=1) for p in (lg_hi, lg_lo)],
                             axis=0)
    ex_s[...] = jnp.exp(jnp.dot(a_ref[...], pieces, preferred_element_type=F32))

    def intra(c):
        r, ln = rows[c], slice(c * HEAD_DIM, (c + 1) * HEAD_DIM)
        kk = 1.0 - g[r]
        q = _silu(hq_ref[r, :].astype(F32))
        scores = _dot_nt(q.astype(BF16), kk.astype(BF16)) * mask_ref[0]
        for li, h in enumerate(levels):
            f = ex_s[(2 + li) * chunk:(3 + li) * chunk, ln]
            y = (jnp.where((row & h) != 0, q, kk) * f).astype(BF16)
            scores = scores + _dot_nt(y, y) * mask_ref[1 + li]
        o_intra = jnp.dot(scores.astype(BF16), hi_ref[r, :], preferred_element_type=F32)
        q_in = (q * ex_s[0:chunk, ln]).astype(BF16)
        k_end = (kk * ex_s[chunk:2 * chunk, ln]).astype(BF16)
        kv = _dot_tn(hi_ref[r, :], k_end)
        return o_intra, q_in, kv

    def finish(c, o_intra, q_in, kv):
        r, ln = rows[c], slice(c * HEAD_DIM, (c + 1) * HEAD_DIM)
        st = st_s[...]
        o = o_intra + _dot_nt(q_in, st.astype(BF16))
        st_s[...] = st * ex_s[chunk - 1:chunk, ln] + kv
        o = _rms(o, gn) * _silu(hog_ref[r, :].astype(F32))
        o_ref[r, :] = o.astype(o_ref.dtype)

    ahead = intra(0)
    for c in range(n_chunks):
        cur, ahead = ahead, (intra(c + 1) if c + 1 < n_chunks else None)
        finish(c, *cur)


def _hgrn(proj, hgrn_lb, layer, gnorm, n_heads, col0, chunk, tt):
    bsz, seq, _ = proj.shape
    a_mat, masks = _hgrn_constants(chunk)
    n_slots = hgrn_lb.shape[0]

    def col(k):
        return pl.BlockSpec((None, tt, HEAD_DIM), lambda b, h, t, k=k: (b, t, col0 + k * n_heads + h))

    return pl.pallas_call(
        functools.partial(_hgrn_body, chunk=chunk, n_chunks=tt // chunk, layer=layer),
        grid=(bsz, n_heads, seq // tt),
        in_specs=[pl.BlockSpec((n_slots, HEAD_DIM), lambda b, h, t: (0, h)),
                  pl.BlockSpec((1, HEAD_DIM), lambda b, h, t: (0, h)),
                  pl.BlockSpec(a_mat.shape, lambda b, h, t: (0, 0)),
                  pl.BlockSpec(masks.shape, lambda b, h, t: (0, 0, 0)),
                  col(0), col(1), col(2), col(3)],
        out_specs=pl.BlockSpec((None, tt, HEAD_DIM), lambda b, h, t: (b, t, h)),
        out_shape=jax.ShapeDtypeStruct((bsz, seq, n_heads * HEAD_DIM), BF16),
        scratch_shapes=[pltpu.VMEM((HEAD_DIM, HEAD_DIM), F32),
                        pltpu.VMEM((a_mat.shape[0], (tt // chunk) * HEAD_DIM), F32)],
        compiler_params=_params(("parallel", "parallel", "arbitrary")),
        name="hgrn",
    )(hgrn_lb, gnorm, a_mat, masks, proj, proj, proj, proj)


def _split3(x):
    hi = x.astype(BF16).astype(F32)
    r = x - hi
    mid = r.astype(BF16).astype(F32)
    lo = (r - mid).astype(BF16).astype(F32)
    return hi, mid, lo


V_ROWS = HEAD_DIM + 16


def _fox_bias_placement(n_heads):
    width = n_heads * LANES
    assert 6 * n_heads <= LANES
    place = np.zeros((3 * LANES, width + LANES), np.float32)
    const = np.zeros((1, width + LANES), np.float32)
    for h in range(n_heads):
        for p in range(3):
            place[p * LANES + h, h * LANES + 6 * h + p] = 1.0
            place[p * LANES + h, width + 6 * h + 3 + p] = -1.0
        const[0, h * LANES + 6 * h + 3:h * LANES + 6 * h + 6] = 1.0
        const[0, width + 6 * h:width + 6 * h + 3] = 1.0
    return jnp.asarray(place, BF16), jnp.asarray(const, F32)


def _fox_prep_body(u_ref, wff_ref, fb_ref, tri_ref, place_ref, const_ref, v_ref, qa_ref, ka_ref, vt_ref,
                   carry_s, *, n_heads):
    @pl.when(pl.program_id(1) == 0)
    def _():
        carry_s[...] = jnp.zeros_like(carry_s)

    tt = u_ref.shape[0]
    width = n_heads * LANES
    ones = jnp.ones((V_ROWS - HEAD_DIM, tt), BF16)
    for h in range(n_heads):
        vt_ref[h * V_ROWS:h * V_ROWS + HEAD_DIM, :] = (
            v_ref[:, h * LANES:(h + 1) * LANES].astype(F32).T.astype(BF16))
        vt_ref[h * V_ROWS + HEAD_DIM:(h + 1) * V_ROWS, :] = ones
    z = jnp.dot(u_ref[...], wff_ref[...], preferred_element_type=F32) + fb_ref[...]
    log_f = jnp.minimum(z, 0.0) - jnp.log(1.0 + jnp.exp(-jnp.abs(z)))
    tri = tri_ref[...]
    c = carry_s[...]
    for piece in _split3(log_f):
        c = c + jnp.dot(tri, piece.astype(BF16), preferred_element_type=F32)
    carry_s[...] = c[tt - 1:tt, :]
    pieces = jnp.concatenate(_split3(c * LOG2E), axis=1).astype(BF16)
    bias = jnp.dot(pieces, place_ref[...], preferred_element_type=F32) + const_ref[...]
    qa_ref[...] = bias[:, :width].astype(BF16)
    ka_ref[...] = bias[:, width:].astype(BF16)


def _fox_prep(u, w_ff, fox_b, proj, col_v, n_heads, tt):
    bsz, seq, d = u.shape
    width = n_heads * LANES
    assert col_v % n_heads == 0
    tri = jnp.asarray(np.tril(np.ones((tt, tt), np.float32)), BF16)
    place, const = _fox_bias_placement(n_heads)
    out = jax.ShapeDtypeStruct((bsz, seq, width), BF16)
    blk = pl.BlockSpec((None, tt, width), lambda b, t: (b, t, 0))
    whole = lambda a: pl.BlockSpec(a.shape, lambda b, t: (0, 0))
    return pl.pallas_call(
        functools.partial(_fox_prep_body, n_heads=n_heads),
        grid=(bsz, seq // tt),
        in_specs=[pl.BlockSpec((None, tt, d), lambda b, t: (b, t, 0)),
                  whole(w_ff), whole(fox_b), whole(tri), whole(place), whole(const),
                  pl.BlockSpec((None, tt, width), lambda b, t: (b, t, col_v // n_heads))],
        out_specs=[blk, pl.BlockSpec((None, tt, LANES), lambda b, t: (b, t, 0)),
                   pl.BlockSpec((None, n_heads * V_ROWS, tt), lambda b, t: (b, 0, t))],
        out_shape=[out, jax.ShapeDtypeStruct((bsz, seq, LANES), BF16),
                   jax.ShapeDtypeStruct((bsz, n_heads * V_ROWS, seq), BF16)],
        scratch_shapes=[pltpu.VMEM((1, LANES), F32)],
        compiler_params=_params(("parallel", "arbitrary")),
        name="fox_prep",
    )(u, w_ff, fox_b, tri, place, const, proj)


def _fox_body(q_ref, qa_ref, k_ref, ka_ref, vt_ref, o_ref, acc_s, st_s, *, tile, group):
    qi = pl.program_id(2)
    lanes = [slice(g * HEAD_DIM, (g + 1) * HEAD_DIM) for g in range(group)]
    qcs = [jnp.concatenate([q_ref[:, ln], qa_ref[:, ln]], axis=1) for ln in lanes]

    def logits(g, j):
        ks = pl.ds(pl.multiple_of(j * tile, tile), tile)
        kc = jnp.concatenate([k_ref[ks, lanes[g]], ka_ref[ks, :]], axis=1)
        return _dot_nt(kc, qcs[g])

    def accumulate(g, j, m, alpha):
        ks = pl.ds(pl.multiple_of(j * tile, tile), tile)
        pt = jnp.exp2(st_s[g] - m).astype(BF16)
        pv = jnp.dot(vt_ref[g * V_ROWS:(g + 1) * V_ROWS, ks], pt, preferred_element_type=F32)
        acc_s[g] = alpha * acc_s[g] + pv

    acc_s[...] = jnp.zeros_like(acc_s)
    key = lax.broadcasted_iota(jnp.int32, (tile, tile), 0)
    qry = lax.broadcasted_iota(jnp.int32, (tile, tile), 1)
    carry = []
    for g in range(group):
        st = jnp.where(key <= qry, logits(g, qi), NEG_BIG)
        st_s[g] = st
        carry.append((jnp.max(st, axis=0, keepdims=True), jnp.ones((1, tile), F32)))

    def step(i, carry):
        prev = jnp.where(i == 0, qi, i - 1)
        out = []
        for g in range(group):
            m, alpha = carry[g]
            st = logits(g, i)
            accumulate(g, prev, m, alpha)
            m_new = jnp.maximum(m, jnp.max(st, axis=0, keepdims=True))
            st_s[g] = st
            out.append((m_new, jnp.exp2(m - m_new)))
        return tuple(out)

    carry = lax.fori_loop(0, qi, step, tuple(carry))
    last = jnp.where(qi == 0, qi, qi - 1)
    for g, ln in enumerate(lanes):
        accumulate(g, last, *carry[g])
        acc = acc_s[g]
        o_ref[:, ln] = (acc[:HEAD_DIM] / acc[HEAD_DIM:HEAD_DIM + 1]).T.astype(o_ref.dtype)


def _fox(proj, qa, ka, vt, n_heads, col_q, col_k, tile, group):
    bsz, seq, _ = proj.shape
    width = group * HEAD_DIM
    assert n_heads % group == 0 and col_q % group == 0 and col_k % group == 0
    qblk = lambda c0: pl.BlockSpec((None, tile, width), lambda b, h, i: (b, i, c0 // group + h))
    kblk = lambda c0: pl.BlockSpec((None, seq, width), lambda b, h, i: (b, 0, c0 // group + h),
                                   pipeline_mode=pl.Buffered(1))
    return pl.pallas_call(
        functools.partial(_fox_body, tile=tile, group=group),
        grid=(bsz, n_heads // group, seq // tile),
        in_specs=[qblk(col_q), qblk(0), kblk(col_k),
                  pl.BlockSpec((None, seq, LANES), lambda b, h, i: (b, 0, 0), pipeline_mode=pl.Buffered(1)),
                  pl.BlockSpec((None, group * V_ROWS, seq), lambda b, h, i: (b, h, 0),
                               pipeline_mode=pl.Buffered(1))],
        out_specs=pl.BlockSpec((None, tile, width), lambda b, h, i: (b, i, h)),
        out_shape=jax.ShapeDtypeStruct((bsz, seq, n_heads * HEAD_DIM), BF16),
        scratch_shapes=[pltpu.VMEM((group, V_ROWS, tile), F32), pltpu.VMEM((group, tile, tile), F32)],
        compiler_params=_params(("parallel", "parallel", "arbitrary")),
        name="fox",
    )(proj, qa, proj, ka, vt)


def _mem_attn_body(u_ref, wq_ref, mem_ref, g_ref, wkv_ref, o_ref, kv_ref, *, n_heads):
    width = n_heads * HEAD_DIM
    scale = HEAD_DIM ** -0.5

    @pl.when(pl.program_id(1) == 0)
    def _():
        mn = _rms(mem_ref[...], g_ref[...]).astype(BF16)
        kv_ref[...] = jnp.dot(mn, wkv_ref[...], preferred_element_type=F32).astype(BF16)

    q = (jnp.dot(u_ref[...], wq_ref[...], preferred_element_type=F32) * scale).astype(BF16)
    heads = [slice(h * HEAD_DIM, (h + 1) * HEAD_DIM) for h in range(n_heads)]
    scores = [_dot_nt(q[:, hd], kv_ref[:, hd]) for hd in heads]
    for hd, s in zip(heads, scores):
        p = jnp.exp(s - jnp.max(s, axis=1, keepdims=True))
        p = p / jnp.sum(p, axis=1, keepdims=True)
        o = jnp.dot(p.astype(BF16), kv_ref[:, width + hd.start:width + hd.stop],
                    preferred_element_type=F32)
        o_ref[:, hd] = o.astype(o_ref.dtype)


def _mem_attn(u, w_mq, mem, mem_g, w_kv, n_heads, tq):
    bsz, seq, d = u.shape
    width = n_heads * HEAD_DIM
    m = mem.shape[1]
    return pl.pallas_call(
        functools.partial(_mem_attn_body, n_heads=n_heads),
        grid=(bsz, seq // tq),
        in_specs=[pl.BlockSpec((None, tq, d), lambda b, i: (b, i, 0)),
                  pl.BlockSpec((d, width), lambda b, i: (0, 0)),
                  pl.BlockSpec((None, m, d), lambda b, i: (b, 0, 0)),
                  pl.BlockSpec((1, d), lambda b, i: (0, 0)),
                  pl.BlockSpec((d, 2 * width), lambda b, i: (0, 0))],
        out_specs=pl.BlockSpec((None, tq, width), lambda b, i: (b, i, 0)),
        out_shape=jax.ShapeDtypeStruct((bsz, seq, width), BF16),
        scratch_shapes=[pltpu.VMEM((m, 2 * width), BF16)],
        compiler_params=_params(("parallel", "arbitrary")),
        name="mem_attn",
    )(u, w_mq, mem, mem_g, w_kv)


def _merge_body(oh_ref, of_ref, om_ref, g0_ref, g1_ref, g2_ref, x_ref, post_ref,
                wh_ref, wf_ref, wm_ref, wo_ref, out_ref):
    merged = g0_ref[...].astype(F32) * jnp.dot(oh_ref[...], wh_ref[...], preferred_element_type=F32)
    merged += g1_ref[...].astype(F32) * jnp.dot(of_ref[...], wf_ref[...], preferred_element_type=F32)
    merged += g2_ref[...].astype(F32) * jnp.dot(om_ref[...], wm_ref[...], preferred_element_type=F32)
    y = jnp.dot(merged.astype(BF16), wo_ref[...], preferred_element_type=F32)
    out_ref[...] = x_ref[...] + _rms(y, post_ref[...])


def _merge(o_h, o_f, o_m, gates, x1, post, wh, wf, wm, wo, tm):
    n, d = x1.shape
    row = lambda width: pl.BlockSpec((tm, width), lambda i: (i, 0))
    gate = lambda k: pl.BlockSpec((tm, d), lambda i, k=k: (i, k))
    full = lambda w: pl.BlockSpec(w.shape, lambda i: (0, 0), pipeline_mode=pl.Buffered(1))
    return pl.pallas_call(
        _merge_body,
        grid=(n // tm,),
        in_specs=[row(o_h.shape[1]), row(o_f.shape[1]), row(o_m.shape[1]),
                  gate(0), gate(1), gate(2), row(d),
                  pl.BlockSpec((1, d), lambda i: (0, 0)),
                  full(wh), full(wf), full(wm), full(wo)],
        out_specs=row(d),
        out_shape=jax.ShapeDtypeStruct((n, d), F32),
        compiler_params=_params(("parallel",)),
        name="merge",
    )(o_h, o_f, o_m, gates, gates, gates, x1, post, wh, wf, wm, wo)


def _pad_cols(w, to):
    return jnp.pad(w, ((0, 0), (0, to - w.shape[1])))


def _ffn_weights(wg, wu, wd):
    return wg.astype(BF16), wu.astype(BF16), wd.astype(BF16)


def kernel(x, mem, ffn1_pre, ffn1_post, ffn1_wg, ffn1_wu, ffn1_wd, mix_pre, mix_post, mem_norm,
           w_in, hgrn_lb, hgrn_gnorm, fox_fb, w_mem_kv, w_hgrn_out, w_fox_out, w_mem_out, w_gate,
           w_o, ffn2_pre, ffn2_post, ffn2_wg, ffn2_wu, ffn2_wd):
    bsz, seq, d = x.shape
    n = bsz * seq
    depth = ffn1_pre.shape[0]
    wh, wf, wm = w_hgrn_out.shape[1], w_fox_out.shape[1], w_mem_out.shape[1]
    h_hgrn, h_fox, h_mem = wh // HEAD_DIM, wf // HEAD_DIM, wm // HEAD_DIM
    c_fox = 4 * wh
    c_ff = c_fox + 3 * wf
    c_mq = c_ff + h_fox
    n_gate = w_gate.shape[2]
    assert h_fox <= LANES and n_gate == 3 * d
    blk = lambda c: c // HEAD_DIM
    t = _tiles(n, seq, c_ff, n_gate, h_fox, (blk(c_fox), blk(c_fox + wf)))
    scale = np.ones((1, c_ff), np.float32)
    scale[:, c_fox:c_fox + wf] = HEAD_DIM ** -0.5 * LOG2E
    scale = jnp.asarray(scale)
    no_scale = jnp.ones((1, n_gate), F32)

    x2 = x.reshape(n, d)
    vec = lambda a: a.reshape(1, -1)
    for l in range(depth):
        wg1, wu1, wd1 = _ffn_weights(ffn1_wg[l], ffn1_wu[l], ffn1_wd[l])
        x1, u = _ffn(x2, vec(ffn1_pre[l]), vec(ffn1_post[l]), vec(mix_pre[l]), wg1, wu1, wd1,
                     t["ffn_tm"], t["ffn_tf"])
        u3 = u.reshape(bsz, seq, d)

        proj = _proj(u, w_in, l, c_ff, scale, False, t["proj_tm"], t["proj_tn"], "in_proj")
        gates = _proj(u, w_gate, l, n_gate, no_scale, True, t["proj_tm"], t["gate_tn"], "gate_proj")
        proj3 = proj.reshape(bsz, seq, -1)

        o_h = _hgrn(proj3, hgrn_lb, l, vec(hgrn_gnorm[l]), h_hgrn, 0, t["hgrn_chunk"], t["hgrn_tt"])

        w_tail = w_in[l][:, c_ff:]
        w_ff = _pad_cols(w_tail[:, :h_fox], LANES).astype(BF16)
        qa, ka, vt = _fox_prep(u3, w_ff, _pad_cols(vec(fox_fb[l]), LANES), proj3,
                               blk(c_fox + 2 * wf), h_fox, t["prep_tt"])
        o_f = _fox(proj3, qa, ka, vt, h_fox, blk(c_fox), blk(c_fox + wf), t["fox_t"], t["fox_group"])

        o_m = _mem_attn(u3, w_tail[:, h_fox:].astype(BF16), mem, vec(mem_norm[l]),
                        w_mem_kv[l].astype(BF16), h_mem, t["mem_tq"])

        x2 = _merge(o_h.reshape(n, wh), o_f.reshape(n, wf), o_m.reshape(n, wm), gates,
                    x1, vec(mix_post[l]), w_hgrn_out[l].astype(BF16), w_fox_out[l].astype(BF16),
                    w_mem_out[l].astype(BF16), w_o[l].astype(BF16), t["merge_tm"])

        wg2, wu2, wd2 = _ffn_weights(ffn2_wg[l], ffn2_wu[l], ffn2_wd[l])
        x2 = _ffn(x2, vec(ffn2_pre[l]), vec(ffn2_post[l]), None, wg2, wu2, wd2,
                  t["ffn_tm"], t["ffn_tf"])
    return x2.reshape(bsz, seq, d)
```
